```python
import math
import jax, jax.numpy as jnp
from jax import lax
import numpy as np

D_MODEL = 4096
BATCH = 4
SEQ = 2048
DEPTH = 1

D_MIX = D_MODEL
ATT_WIDTH = D_MIX // 2
SSM_WIDTH = D_MIX - ATT_WIDTH

ATT_HEAD_DIM = 128
N_HEADS = ATT_WIDTH // ATT_HEAD_DIM
N_KV_HEADS = 4
KV_GROUP = N_HEADS // N_KV_HEADS
MOBA_BLOCK = 256
MOBA_TOPK = 3
Q_CHUNK = 16
ROPE_THETA = 10000.0

SSM_HEAD_DIM = 64
SSM_HEADS = SSM_WIDTH // SSM_HEAD_DIM
SSM_GROUPS = 4
SSM_STATE = 128
SSM_CHUNK = 256
CONV_WIDTH = 4
CONV_DIM = SSM_WIDTH + 2 * SSM_GROUPS * SSM_STATE

D_FF = -(-8 * D_MODEL // (3 * 256)) * 256

EPS = 1e-6

Q_DIM = N_HEADS * ATT_HEAD_DIM
KV_DIM = N_KV_HEADS * ATT_HEAD_DIM
IN_COLS = Q_DIM + 2 * KV_DIM + SSM_WIDTH + CONV_DIM + SSM_HEADS
IN_SPLITS = [Q_DIM, Q_DIM + KV_DIM, Q_DIM + 2 * KV_DIM,
             Q_DIM + 2 * KV_DIM + SSM_WIDTH,
             Q_DIM + 2 * KV_DIM + SSM_WIDTH + CONV_DIM]

kernel_name = 'hybrid_moba_mamba2_block'


def _round_up(n, m):
    return -(-n // m) * m


def _pad_seq(t, s_pad):
    pad = s_pad - t.shape[1]
    if pad == 0:
        return t
    widths = [(0, 0)] * t.ndim
    widths[1] = (0, pad)
    return jnp.pad(t, widths)


def rms_norm(x, w):
    x32 = x.astype(jnp.float32)
    y = x32 * lax.rsqrt(jnp.mean(x32 * x32, axis=-1, keepdims=True) + EPS)
    return (y * w.astype(jnp.float32)).astype(x.dtype)


def rope(t, positions):
    half = t.shape[-1] // 2
    inv_freq = ROPE_THETA ** (-jnp.arange(half, dtype=jnp.float32) / half)
    ang = positions.astype(jnp.float32)[..., None] * inv_freq
    cos = jnp.cos(ang)[:, :, None, :]
    sin = jnp.sin(ang)[:, :, None, :]
    t32 = t.astype(jnp.float32)
    t1, t2 = t32[..., :half], t32[..., half:]
    return jnp.concatenate([t1 * cos - t2 * sin, t2 * cos + t1 * sin], axis=-1).astype(t.dtype)


def moba_attention(q, k, v):
    b, s, _, dh = q.shape
    s_pad = _round_up(s, MOBA_BLOCK)
    q, k, v = _pad_seq(q, s_pad), _pad_seq(k, s_pad), _pad_seq(v, s_pad)
    nb = s_pad // MOBA_BLOCK
    n_sel = min(MOBA_TOPK, nb - 1)
    scale = dh ** -0.5
    qg = q.reshape(b, s_pad, N_KV_HEADS, KV_GROUP, dh).transpose(0, 2, 3, 1, 4)
    kb = k.transpose(0, 2, 1, 3).reshape(b, N_KV_HEADS, nb, MOBA_BLOCK, dh)
    vb = v.transpose(0, 2, 1, 3).reshape(b, N_KV_HEADS, nb, MOBA_BLOCK, dh)
    k_mean = jnp.mean(kb.astype(jnp.float32), axis=3)
    blk_ids = jnp.arange(nb)
    b_idx = jnp.arange(b)[:, None, None, None, None]
    h_idx = jnp.arange(N_KV_HEADS)[None, :, None, None, None]

    def chunk(start):
        qc = lax.dynamic_slice_in_dim(qg, start, Q_CHUNK, axis=3) * scale
        own = start // MOBA_BLOCK
        qpos = start + jnp.arange(Q_CHUNK)
        k_own = lax.dynamic_index_in_dim(kb, own, axis=2, keepdims=False)
        v_own = lax.dynamic_index_in_dim(vb, own, axis=2, keepdims=False)
        kpos = own * MOBA_BLOCK + jnp.arange(MOBA_BLOCK)
        s_own = jnp.einsum('bkgcd,bktd->bkgct', qc, k_own).astype(jnp.float32)
        s_own = jnp.where(kpos[None, :] <= qpos[:, None], s_own, -jnp.inf)
        if n_sel == 0:
            p_own = jax.nn.softmax(s_own, axis=-1).astype(v.dtype)
            return jnp.einsum('bkgct,bktd->bkgcd', p_own, v_own)
        gate = jnp.einsum('bkgcd,bknd->bkgcn', qc.astype(jnp.float32), k_mean)
        gate = jnp.where(blk_ids < own, gate, -jnp.inf)
        _, sel = lax.top_k(gate, n_sel)
        valid = sel < own
        k_sel = kb[b_idx, h_idx, sel]
        v_sel = vb[b_idx, h_idx, sel]
        s_past = jnp.einsum('bkgcd,bkgcjtd->bkgcjt', qc, k_sel).astype(jnp.float32)
        s_past = jnp.where(valid[..., None], s_past, -jnp.inf)
        s_past = s_past.reshape(s_past.shape[:4] + (n_sel * MOBA_BLOCK,))
        p = jax.nn.softmax(jnp.concatenate([s_past, s_own], axis=-1), axis=-1).astype(v.dtype)
        p_past = p[..., :n_sel * MOBA_BLOCK].reshape(p.shape[:4] + (n_sel, MOBA_BLOCK))
        p_own = p[..., n_sel * MOBA_BLOCK:]
        return (jnp.einsum('bkgcjt,bkgcjtd->bkgcd', p_past, v_sel)
                + jnp.einsum('bkgct,bktd->bkgcd', p_own, v_own))

    starts = jnp.arange(0, s_pad, Q_CHUNK)
    outs = lax.map(chunk, starts)
    out = outs.transpose(1, 0, 4, 2, 3, 5).reshape(b, s_pad, N_HEADS * dh)
    return out[:, :s]


def causal_depthwise_conv(u, w, bias):
    out = lax.conv_general_dilated(
        u, w[:, None, :].astype(u.dtype), window_strides=(1,),
        padding=[(CONV_WIDTH - 1, 0)], dimension_numbers=('NWC', 'WIO', 'NWC'),
        feature_group_count=u.shape[-1])
    return out + bias.astype(u.dtype)


def ssd_chunked(xs, dt, a, bm, cm):
    b, s = xs.shape[:2]
    s_pad = _round_up(s, SSM_CHUNK)
    nc = s_pad // SSM_CHUNK
    L = SSM_CHUNK
    g = SSM_GROUPS
    hg = SSM_HEADS // g
    f32 = jnp.float32
    xs = _pad_seq(xs.astype(f32), s_pad)
    dt = _pad_seq(dt, s_pad)
    bm = _pad_seq(bm.astype(f32), s_pad)
    cm = _pad_seq(cm.astype(f32), s_pad)
    x_dt = (xs * dt[..., None]).reshape(b, nc, L, g, hg, SSM_HEAD_DIM)
    a_dt = (dt * a).reshape(b, nc, L, g, hg).transpose(0, 3, 4, 1, 2)
    a_cum = jnp.cumsum(a_dt, axis=-1)
    bc = bm.reshape(b, nc, L, g, SSM_STATE)
    cc = cm.reshape(b, nc, L, g, SSM_STATE)
    causal = jnp.tril(jnp.ones((L, L), dtype=bool))
    seg = a_cum[..., :, None] - a_cum[..., None, :]
    decay = jnp.exp(jnp.where(causal, seg, -jnp.inf))
    cb = jnp.einsum('bclgn,bcsgn->bgcls', cc, bc)
    y_diag = jnp.einsum('bgcls,bghcls,bcsghp->bclghp', cb, decay, x_dt)
    state_decay = jnp.exp(a_cum[..., -1:] - a_cum)
    states = jnp.einsum('bcsgn,bghcs,bcsghp->bcghpn', bc, state_decay, x_dt)
    chunk_decay = jnp.exp(a_cum[..., -1])

    def step(h, inp):
        st, dec = inp
        return dec[..., None, None] * h + st, h

    h0 = jnp.zeros((b, g, hg, SSM_HEAD_DIM, SSM_STATE), f32)
    _, prev = lax.scan(step, h0, (jnp.moveaxis(states, 1, 0), jnp.moveaxis(chunk_decay, 3, 0)))
    prev = jnp.moveaxis(prev, 0, 1)
    y_off = jnp.einsum('bclgn,bcghpn,bghcl->bclghp', cc, prev, jnp.exp(a_cum))
    y = (y_diag + y_off).reshape(b, s_pad, SSM_HEADS, SSM_HEAD_DIM)
    return y[:, :s]


def mamba2_mixer(z, xbc, dt_raw, conv_w, conv_b, dt_bias, a_log, d_skip, norm_w):
    b, s, _ = z.shape
    xbc = jax.nn.silu(causal_depthwise_conv(xbc, conv_w, conv_b))
    xs, bm, cm = jnp.split(xbc, [SSM_WIDTH, SSM_WIDTH + SSM_GROUPS * SSM_STATE], axis=-1)
    xs = xs.reshape(b, s, SSM_HEADS, SSM_HEAD_DIM)
    bm = bm.reshape(b, s, SSM_GROUPS, SSM_STATE)
    cm = cm.reshape(b, s, SSM_GROUPS, SSM_STATE)
    dt = jax.nn.softplus(dt_raw.astype(jnp.float32) + dt_bias.astype(jnp.float32))
    a = -jnp.exp(a_log.astype(jnp.float32))
    y = ssd_chunked(xs, dt, a, bm, cm)
    y = y + xs.astype(jnp.float32) * d_skip.astype(jnp.float32)[:, None]
    y = y.reshape(b, s, SSM_WIDTH) * jax.nn.silu(z.astype(jnp.float32))
    yg = y.reshape(b, s, SSM_GROUPS, SSM_WIDTH // SSM_GROUPS)
    yg = yg * lax.rsqrt(jnp.mean(yg * yg, axis=-1, keepdims=True) + EPS)
    return (yg.reshape(b, s, SSM_WIDTH) * norm_w.astype(jnp.float32)).astype(z.dtype)


def setup_inputs(seed: int = 0) -> dict:
    key = jax.random.key(seed)
    ks = jax.random.split(key, 20)
    f32 = jnp.float32

    def nrm(k, shape, fan_in):
        return jax.random.normal(k, shape, f32) * fan_in ** -0.5

    def gain(k, n):
        return 1.0 + 0.05 * jax.random.normal(k, (DEPTH, n), f32)

    x = jax.random.normal(ks[0], (BATCH, SEQ, D_MODEL), f32)
    positions = jnp.broadcast_to(jnp.arange(SEQ, dtype=jnp.int32), (BATCH, SEQ))
    dt0 = jnp.exp(jax.random.uniform(ks[6], (DEPTH, SSM_HEADS), f32, math.log(1e-3), math.log(1e-1)))
    return {
        'x': x,
        'positions': positions,
        'mix_pre_norm': gain(ks[1], D_MODEL),
        'w_in': nrm(ks[2], (DEPTH, D_MODEL, IN_COLS), D_MODEL),
        'conv_w': nrm(ks[3], (DEPTH, CONV_WIDTH, CONV_DIM), CONV_WIDTH),
        'conv_b': 0.02 * jax.random.normal(ks[4], (DEPTH, CONV_DIM), f32),
        'dt_bias': dt0 + jnp.log(-jnp.expm1(-dt0)),
        'a_log': jnp.log(jax.random.uniform(ks[7], (DEPTH, SSM_HEADS), f32, 1.0, 16.0)),
        'd_skip': 1.0 + 0.1 * jax.random.normal(ks[8], (DEPTH, SSM_HEADS), f32),
        'ssm_norm': gain(ks[9], SSM_WIDTH),
        'w_out': nrm(ks[10], (DEPTH, D_MIX, D_MODEL), D_MIX),
        'mix_post_norm': gain(ks[11], D_MODEL),
        'ffn_pre_norm': gain(ks[12], D_MODEL),
        'w_gate': nrm(ks[13], (DEPTH, D_MODEL, D_FF), D_MODEL),
        'w_up': nrm(ks[14], (DEPTH, D_MODEL, D_FF), D_MODEL),
        'w_down': nrm(ks[15], (DEPTH, D_FF, D_MODEL), D_FF),
        'ffn_post_norm': gain(ks[16], D_MODEL),
    }


def reference(x, positions, mix_pre_norm, w_in, conv_w, conv_b, dt_bias, a_log, d_skip,
              ssm_norm, w_out, mix_post_norm, ffn_pre_norm, w_gate, w_up, w_down,
              ffn_post_norm):
    b, s, _ = x.shape
    for layer in range(DEPTH):
        h = rms_norm(x, mix_pre_norm[layer])
        proj = h @ w_in[layer]
        q, k, v, z, xbc, dt_raw = jnp.split(proj, IN_SPLITS, axis=-1)
        q = rope(q.reshape(b, s, N_HEADS, ATT_HEAD_DIM), positions)
        k = rope(k.reshape(b, s, N_KV_HEADS, ATT_HEAD_DIM), positions)
        v = v.reshape(b, s, N_KV_HEADS, ATT_HEAD_DIM)
        att = moba_attention(q, k, v)
        ssm = mamba2_mixer(z, xbc, dt_raw, conv_w[layer], conv_b[layer], dt_bias[layer],
                           a_log[layer], d_skip[layer], ssm_norm[layer])
        mixed = jnp.concatenate([att, ssm], axis=-1) @ w_out[layer]
        x = x + rms_norm(mixed, mix_post_norm[layer])
        h = rms_norm(x, ffn_pre_norm[layer])
        f = (jax.nn.silu(h @ w_gate[layer]) * (h @ w_up[layer])) @ w_down[layer]
        x = x + rms_norm(f, ffn_post_norm[layer])
    return x
```

```python
import functools
import math

import jax
import jax.numpy as jnp
from jax import lax
from jax.experimental import pallas as pl
from jax.experimental.pallas import tpu as pltpu

F32 = jnp.float32
BF16 = jnp.bfloat16

D_MODEL = 4096
N_HEADS = 16
N_KV_HEADS = 4
KV_GROUP = N_HEADS // N_KV_HEADS
HEAD_DIM = 128
MOBA_BLOCK = 256
MOBA_TOPK = 3
ROPE_THETA = 10000.0
ATT_WIDTH = N_HEADS * HEAD_DIM
KV_DIM = N_KV_HEADS * HEAD_DIM
SSM_WIDTH = 2048
SSM_HEAD_DIM = 64
SSM_HEADS = SSM_WIDTH // SSM_HEAD_DIM
SSM_GROUPS = 4
HEADS_PER_GROUP = SSM_HEADS // SSM_GROUPS
GROUP_WIDTH = SSM_WIDTH // SSM_GROUPS
SSM_STATE = 128
SSM_CHUNK = 256
CONV_WIDTH = 4
D_FF = 11008
EPS = 1e-6

Q_OFF = 0
K_OFF = ATT_WIDTH
V_OFF = K_OFF + KV_DIM
Z_OFF = V_OFF + KV_DIM
XS_OFF = Z_OFF + SSM_WIDTH
BM_OFF = XS_OFF + SSM_WIDTH
CM_OFF = BM_OFF + SSM_GROUPS * SSM_STATE
DT_OFF = CM_OFF + SSM_GROUPS * SSM_STATE
PROJ_COLS = DT_OFF

V7X_LANES = 128
V7X_SUBLANES = 8
V7X_VMEM_LIMIT_BYTES = 60000 * 1024

MASK_BIAS = -1e30


def _params(n_axes, vmem_bytes=V7X_VMEM_LIMIT_BYTES):
    return pltpu.CompilerParams(dimension_semantics=("arbitrary",) * n_axes,
                                vmem_limit_bytes=vmem_bytes)


def _sigmoid(x):
    return 1.0 / (1.0 + jnp.exp(-x))


def _silu(x):
    return x * _sigmoid(x)


def _split3(x):
    hi = x.astype(BF16)
    r1 = x - hi.astype(F32)
    mid = r1.astype(BF16)
    lo = (r1 - mid.astype(F32)).astype(BF16)
    return hi, mid, lo


def _rope_kernel(pos_ref, invf_ref, cos_ref, sin_ref):
    ang = pos_ref[...].astype(F32) * invf_ref[...]
    lane = lax.broadcasted_iota(jnp.int32, ang.shape, 1)
    cos_ref[...] = jnp.cos(ang)
    sin_ref[...] = jnp.where(lane < HEAD_DIM // 2, -jnp.sin(ang), jnp.sin(ang))


def _rope_tables(positions):
    n = positions.size
    half = HEAD_DIM // 2
    inv_freq = ROPE_THETA ** (-jnp.arange(half, dtype=F32) / half)
    invf = jnp.concatenate([inv_freq, inv_freq])[None, :]
    pos = positions.reshape(n, 1)
    tm = 1024
    return pl.pallas_call(
        _rope_kernel,
        grid=(n // tm,),
        in_specs=[pl.BlockSpec((tm, 1), lambda i: (i, 0)),
                  pl.BlockSpec((1, HEAD_DIM), lambda i: (0, 0))],
        out_specs=[pl.BlockSpec((tm, HEAD_DIM), lambda i: (i, 0)),
                   pl.BlockSpec((tm, HEAD_DIM), lambda i: (i, 0))],
        out_shape=[jax.ShapeDtypeStruct((n, HEAD_DIM), F32)] * 2,
        compiler_params=_params(1),
        name="rope_tables",
    )(pos, invf)


def _rope(t, cos, sin_signed):
    return t * cos + pltpu.roll(t, HEAD_DIM // 2, 1) * sin_signed


def _inproj_kernel(x_ref, nw_ref, w_ref, wdt_ref, proj_ref, dtT_ref, h_ref):
    @pl.when(pl.program_id(1) == 0)
    def _():
        x = x_ref[...]
        h = x * lax.rsqrt(jnp.mean(x * x, axis=-1, keepdims=True) + EPS) * nw_ref[...]
        hb = h.astype(BF16)
        h_ref[...] = hb
        dtT_ref[...] = lax.dot_general(wdt_ref[...], hb, (((1,), (1,)), ((), ())),
                                       preferred_element_type=F32)

    proj_ref[...] = jnp.dot(h_ref[...], w_ref[...], preferred_element_type=F32)


def _in_proj(x2d, norm_w, w_in_bf16, wdt_t):
    n_tok = x2d.shape[0]
    tm, tn = 512, 1024
    return pl.pallas_call(
        _inproj_kernel,
        grid=(n_tok // tm, PROJ_COLS // tn),
        in_specs=[pl.BlockSpec((tm, D_MODEL), lambda m, n: (m, 0)),
                  pl.BlockSpec((1, D_MODEL), lambda m, n: (0, 0)),
                  pl.BlockSpec((D_MODEL, tn), lambda m, n: (0, n)),
                  pl.BlockSpec((SSM_HEADS, D_MODEL), lambda m, n: (0, 0))],
        out_specs=[pl.BlockSpec((tm, tn), lambda m, n: (m, n)),
                   pl.BlockSpec((SSM_HEADS, tm), lambda m, n: (0, m))],
        out_shape=[jax.ShapeDtypeStruct((n_tok, PROJ_COLS), F32),
                   jax.ShapeDtypeStruct((SSM_HEADS, n_tok), F32)],
        scratch_shapes=[pltpu.VMEM((tm, D_MODEL), BF16)],
        compiler_params=_params(2),
        name="in_proj",
    )(x2d, norm_w, w_in_bf16, wdt_t)


def _attn_kernel(q_ref, k_ref, v_ref, cq_ref, sq_ref, ck_ref, sk_ref, o_ref,
                 kr_ref, km_ref, vb_ref, *, n_blocks):
    i = pl.program_id(2)
    rows = KV_GROUP * MOBA_BLOCK
    scale = HEAD_DIM ** -0.5

    @pl.when(i == 0)
    def _():
        kr = _rope(k_ref[...], ck_ref[...], sk_ref[...])
        kr_ref[...] = kr.astype(BF16)
        vb_ref[...] = v_ref[...].astype(BF16)
        km_ref[...] = jnp.mean(kr.reshape(n_blocks, MOBA_BLOCK, HEAD_DIM), axis=1)

    cq = cq_ref[...]
    sq = sq_ref[...]
    q_all = jnp.concatenate(
        [_rope(q_ref[:, g * HEAD_DIM:(g + 1) * HEAD_DIM], cq, sq) * scale for g in range(KV_GROUP)],
        axis=0)
    q_bf = q_all.astype(BF16)

    km = km_ref[...]
    q_hi, q_mid, _ = _split3(q_all)
    km_hi, km_mid, _ = _split3(km)
    nt = (((1,), (1,)), ((), ()))
    gate_t = (lax.dot_general(km_hi, q_hi, nt, preferred_element_type=F32)
              + lax.dot_general(km_hi, q_mid, nt, preferred_element_type=F32)
              + lax.dot_general(km_mid, q_hi, nt, preferred_element_type=F32))
    blk = lax.broadcasted_iota(jnp.int32, gate_t.shape, 0)
    past = blk < i
    gate_t = jnp.where(past, gate_t, -jnp.inf)
    rank = jnp.zeros(gate_t.shape, F32)
    for jp in range(n_blocks):
        other = gate_t[jp:jp + 1, :]
        beats = jnp.where(other > gate_t, 1.0, jnp.where((other == gate_t) & (jp < blk), 1.0, 0.0))
        rank = rank + beats
    bias_t = jnp.where(past & (rank < MOBA_TOPK), 0.0, MASK_BIAS)
    bias_t = jnp.concatenate(
        [bias_t, jnp.zeros((V7X_LANES - n_blocks, rows), F32)], axis=0)
    bias = bias_t.T.astype(BF16)
    q_aug = jnp.concatenate([q_bf, bias], axis=1)

    own = pl.multiple_of(i * MOBA_BLOCK, MOBA_BLOCK)
    k_own = kr_ref[pl.ds(own, MOBA_BLOCK), :]
    s = lax.dot_general(q_bf, k_own, nt, preferred_element_type=F32)
    qpos = lax.broadcasted_iota(jnp.int32, s.shape, 0) & (MOBA_BLOCK - 1)
    kpos = lax.broadcasted_iota(jnp.int32, s.shape, 1)
    s = jnp.where(kpos <= qpos, s, -jnp.inf)
    m0 = jnp.max(s, axis=-1, keepdims=True)
    p = jnp.exp(s - m0)
    l0 = jnp.sum(p, axis=-1, keepdims=True)
    acc0 = jnp.dot(p.astype(BF16), vb_ref[pl.ds(own, MOBA_BLOCK), :], preferred_element_type=F32)

    lane = lax.broadcasted_iota(jnp.int32, (MOBA_BLOCK, V7X_LANES), 1)

    def body(j, carry):
        m, l, acc = carry
        start = pl.multiple_of(j * MOBA_BLOCK, MOBA_BLOCK)
        k_aug = jnp.concatenate(
            [kr_ref[pl.ds(start, MOBA_BLOCK), :], jnp.where(lane == j, 1.0, 0.0).astype(BF16)], axis=1)
        sj = lax.dot_general(q_aug, k_aug, nt, preferred_element_type=F32)
        m_new = jnp.maximum(m, jnp.max(sj, axis=-1, keepdims=True))
        alpha = jnp.exp(m - m_new)
        pj = jnp.exp(sj - m_new)
        l = alpha * l + jnp.sum(pj, axis=-1, keepdims=True)
        acc = alpha * acc + jnp.dot(pj.astype(BF16), vb_ref[pl.ds(start, MOBA_BLOCK), :],
                                    preferred_element_type=F32)
        return m_new, l, acc

    _, l, acc = lax.fori_loop(0, i, body, (m0, l0, acc0))
    out = acc / l
    for g in range(KV_GROUP):
        o_ref[:, g * HEAD_DIM:(g + 1) * HEAD_DIM] = out[g * MOBA_BLOCK:(g + 1) * MOBA_BLOCK].astype(o_ref.dtype)


def _attention(proj, cos, sin, batch, seq):
    n_blocks = seq // MOBA_BLOCK
    qw = KV_GROUP * HEAD_DIM
    kern = functools.partial(_attn_kernel, n_blocks=n_blocks)
    return pl.pallas_call(
        kern,
        grid=(batch, N_KV_HEADS, n_blocks),
        in_specs=[
            pl.BlockSpec((MOBA_BLOCK, qw), lambda b, h, i: (b * n_blocks + i, h)),
            pl.BlockSpec((seq, HEAD_DIM), lambda b, h, i: (b, K_OFF // HEAD_DIM + h)),
            pl.BlockSpec((seq, HEAD_DIM), lambda b, h, i: (b, V_OFF // HEAD_DIM + h)),
            pl.BlockSpec((MOBA_BLOCK, HEAD_DIM), lambda b, h, i: (b * n_blocks + i, 0)),
            pl.BlockSpec((MOBA_BLOCK, HEAD_DIM), lambda b, h, i: (b * n_blocks + i, 0)),
            pl.BlockSpec((seq, HEAD_DIM), lambda b, h, i: (b, 0)),
            pl.BlockSpec((seq, HEAD_DIM), lambda b, h, i: (b, 0)),
        ],
        out_specs=pl.BlockSpec((MOBA_BLOCK, qw), lambda b, h, i: (b * n_blocks + i, h)),
        out_shape=jax.ShapeDtypeStruct((batch * seq, ATT_WIDTH), BF16),
        scratch_shapes=[pltpu.VMEM((seq, HEAD_DIM), BF16),
                        pltpu.VMEM((n_blocks, HEAD_DIM), F32),
                        pltpu.VMEM((seq, HEAD_DIM), BF16)],
        compiler_params=_params(3),
        name="moba_attention",
    )(proj, proj, proj, cos, sin, cos, sin)


def _ssm_kernel(z_ref, xs_ref, bm_ref, cm_ref, dtT_ref,
                cwx_ref, cwb_ref, cwc_ref, cbx_ref, cbb_ref, cbc_ref,
                dtb_ref, alog_ref, dsk_ref, nw_ref, o_ref,
                xt_ref, bt_ref, ct_ref, st_ref):
    c = pl.program_id(2)
    L = SSM_CHUNK
    tail = V7X_SUBLANES

    @pl.when(c == 0)
    def _():
        xt_ref[...] = jnp.zeros_like(xt_ref)
        bt_ref[...] = jnp.zeros_like(bt_ref)
        ct_ref[...] = jnp.zeros_like(ct_ref)
        st_ref[...] = jnp.zeros_like(st_ref)

    def conv_silu(u_ref, tail_ref, w_ref, b_ref):
        u = u_ref[...]
        ext = jnp.concatenate([tail_ref[...], u], axis=0)
        acc = b_ref[...] + w_ref[CONV_WIDTH - 1:CONV_WIDTH, :] * u
        for k in range(CONV_WIDTH - 1):
            lo = tail - (CONV_WIDTH - 1) + k
            acc = acc + w_ref[k:k + 1, :] * ext[lo:lo + L]
        tail_ref[...] = u[L - tail:L]
        return _silu(acc)

    xs = conv_silu(xs_ref, xt_ref, cwx_ref, cbx_ref)
    bm = conv_silu(bm_ref, bt_ref, cwb_ref, cbb_ref)
    cm = conv_silu(cm_ref, ct_ref, cwc_ref, cbc_ref)

    dt_in = dtT_ref[...] + dtb_ref[...]
    dt_t = jnp.maximum(dt_in, 0.0) + jnp.log1p(jnp.exp(-jnp.abs(dt_in)))
    a = -jnp.exp(alog_ref[...])
    acum_t = dt_t * a
    lane = lax.broadcasted_iota(jnp.int32, acum_t.shape, 1)
    shift = 1
    while shift < L:
        acum_t = acum_t + jnp.where(lane >= shift, pltpu.roll(acum_t, shift, 1), 0.0)
        shift *= 2

    hpad = 2 * V7X_SUBLANES

    def expand(v_t, width):
        v16 = jnp.concatenate([v_t, jnp.zeros_like(v_t)], axis=0)
        r = lax.broadcasted_iota(jnp.int32, (hpad, HEADS_PER_GROUP * width), 0)
        col = lax.broadcasted_iota(jnp.int32, (hpad, HEADS_PER_GROUP * width), 1)
        e = jnp.where(col // width == r, 1.0, 0.0).astype(BF16)
        tn = (((0,), (0,)), ((), ()))
        out = None
        for part in _split3(v16):
            d = lax.dot_general(part, e, tn, preferred_element_type=F32)
            out = d if out is None else out + d
        return out

    dt_exp = expand(dt_t, SSM_HEAD_DIM)
    acum_exp = expand(acum_t, SSM_HEAD_DIM)
    acum_col = expand(acum_t, L)

    x_dt = xs * dt_exp
    x_dt_bf = x_dt.astype(BF16)
    bm_bf = bm.astype(BF16)
    cm_bf = cm.astype(BF16)
    nt = (((1,), (1,)), ((), ()))
    cb = lax.dot_general(cm_bf, bm_bf, nt, preferred_element_type=F32)
    li = lax.broadcasted_iota(jnp.int32, (L, L), 0)
    si = lax.broadcasted_iota(jnp.int32, (L, L), 1)
    causal = si <= li
    half = lax.broadcasted_iota(jnp.int32, (L, V7X_LANES), 1) < SSM_HEAD_DIM

    def head_mix(h):
        seg = acum_col[:, h * L:(h + 1) * L] - acum_t[h:h + 1, :]
        return (cb * jnp.exp(jnp.where(causal, seg, -jnp.inf))).astype(BF16)

    y_pairs = []
    for pr in range(HEADS_PER_GROUP // 2):
        x_pair = x_dt_bf[:, pr * V7X_LANES:(pr + 1) * V7X_LANES]
        y_even = jnp.dot(head_mix(2 * pr), x_pair, preferred_element_type=F32)
        y_odd = jnp.dot(head_mix(2 * pr + 1), x_pair, preferred_element_type=F32)
        y_pairs.append(jnp.where(half, y_even, y_odd))
    y_diag = jnp.concatenate(y_pairs, axis=1)

    st_prev = st_ref[...]
    e_cum = jnp.exp(acum_exp)
    y_off = jnp.dot(cm_bf, st_prev.astype(BF16), preferred_element_type=F32) * e_cum
    a_last = acum_exp[L - 1:L, :]
    xw = (x_dt * jnp.exp(a_last - acum_exp)).astype(BF16)
    states_t = lax.dot_general(bm_bf, xw, (((0,), (0,)), ((), ())), preferred_element_type=F32)
    st_ref[...] = st_prev * jnp.exp(a_last) + states_t

    y = y_diag + y_off + xs * dsk_ref[...]
    y = y * _silu(z_ref[...])
    y = y * lax.rsqrt(jnp.mean(y * y, axis=-1, keepdims=True) + EPS) * nw_ref[...]
    o_ref[...] = y.astype(o_ref.dtype)


def _ssm(proj, dt_t, conv_w, conv_b, dt_bias, a_log, d_skip, ssm_norm, batch, seq):
    nc = seq // SSM_CHUNK
    L = SSM_CHUNK
    gw = GROUP_WIDTH
    hg = HEADS_PER_GROUP
    dsk = jnp.repeat(d_skip, SSM_HEAD_DIM)[None, :]
    row = lambda b, g, c: b * nc + c
    in_specs = [
        pl.BlockSpec((L, gw), lambda b, g, c: (row(b, g, c), Z_OFF // gw + g)),
        pl.BlockSpec((L, gw), lambda b, g, c: (row(b, g, c), XS_OFF // gw + g)),
        pl.BlockSpec((L, SSM_STATE), lambda b, g, c: (row(b, g, c), BM_OFF // SSM_STATE + g)),
        pl.BlockSpec((L, SSM_STATE), lambda b, g, c: (row(b, g, c), CM_OFF // SSM_STATE + g)),
        pl.BlockSpec((hg, L), lambda b, g, c: (g, row(b, g, c))),
        pl.BlockSpec((CONV_WIDTH, gw), lambda b, g, c: (0, g)),
        pl.BlockSpec((CONV_WIDTH, SSM_STATE), lambda b, g, c: (0, SSM_WIDTH // SSM_STATE + g)),
        pl.BlockSpec((CONV_WIDTH, SSM_STATE), lambda b, g, c: (0, SSM_WIDTH // SSM_STATE + SSM_GROUPS + g)),
        pl.BlockSpec((1, gw), lambda b, g, c: (0, g)),
        pl.BlockSpec((1, SSM_STATE), lambda b, g, c: (0, SSM_WIDTH // SSM_STATE + g)),
        pl.BlockSpec((1, SSM_STATE), lambda b, g, c: (0, SSM_WIDTH // SSM_STATE + SSM_GROUPS + g)),
        pl.BlockSpec((hg, 1), lambda b, g, c: (g, 0)),
        pl.BlockSpec((hg, 1), lambda b, g, c: (g, 0)),
        pl.BlockSpec((1, gw), lambda b, g, c: (0, g)),
        pl.BlockSpec((1, gw), lambda b, g, c: (0, g)),
    ]
    return pl.pallas_call(
        _ssm_kernel,
        grid=(batch, SSM_GROUPS, nc),
        in_specs=in_specs,
        out_specs=pl.BlockSpec((L, gw), lambda b, g, c: (row(b, g, c), g)),
        out_shape=jax.ShapeDtypeStruct((batch * seq, SSM_WIDTH), BF16),
        scratch_shapes=[pltpu.VMEM((V7X_SUBLANES, gw), F32),
                        pltpu.VMEM((V7X_SUBLANES, SSM_STATE), F32),
                        pltpu.VMEM((V7X_SUBLANES, SSM_STATE), F32),
                        pltpu.VMEM((SSM_STATE, gw), F32)],
        compiler_params=_params(3),
        name="ssd_mixer",
    )(proj, proj, proj, proj, dt_t,
      conv_w, conv_w, conv_w, conv_b, conv_b, conv_b,
      dt_bias.reshape(SSM_HEADS, 1), a_log.reshape(SSM_HEADS, 1), dsk, ssm_norm)


def _outproj_kernel(att_ref, ssm_ref, wa_ref, ws_ref, x_ref, post_ref, x1_ref, *, tn, n_steps):
    n = pl.program_id(1)
    mixed = (jnp.dot(att_ref[...], wa_ref[...], preferred_element_type=F32)
             + jnp.dot(ssm_ref[...], ws_ref[...], preferred_element_type=F32))
    x1_ref[:, pl.ds(pl.multiple_of(n * tn, tn), tn)] = mixed

    @pl.when(n == n_steps - 1)
    def _():
        y = x1_ref[...]
        x1_ref[...] = x_ref[...] + y * lax.rsqrt(jnp.mean(y * y, axis=-1, keepdims=True) + EPS) * post_ref[...]


def _out_proj(att, ssm, w_out_bf16, x2d, post_w):
    n_tok = x2d.shape[0]
    tm, tn = 512, 256
    n_steps = D_MODEL // tn
    kern = functools.partial(_outproj_kernel, tn=tn, n_steps=n_steps)
    return pl.pallas_call(
        kern,
        grid=(n_tok // tm, n_steps),
        in_specs=[pl.BlockSpec((tm, ATT_WIDTH), lambda m, n: (m, 0)),
                  pl.BlockSpec((tm, SSM_WIDTH), lambda m, n: (m, 0)),
                  pl.BlockSpec((ATT_WIDTH, tn), lambda m, n: (0, n)),
                  pl.BlockSpec((SSM_WIDTH, tn), lambda m, n: (1, n)),
                  pl.BlockSpec((tm, D_MODEL), lambda m, n: (m, 0)),
                  pl.BlockSpec((1, D_MODEL), lambda m, n: (0, 0))],
        out_specs=pl.BlockSpec((tm, D_MODEL), lambda m, n: (m, 0)),
        out_shape=jax.ShapeDtypeStruct((n_tok, D_MODEL), F32),
        compiler_params=_params(2),
        name="out_proj",
    )(att, ssm, w_out_bf16, w_out_bf16, x2d, post_w)


def _ffn_kernel(x1_ref, pre_ref, wg_ref, wu_ref, wd_ref, post_ref, o_ref, h_ref, *, n_steps):
    f = pl.program_id(1)

    @pl.when(f == 0)
    def _():
        o_ref[...] = jnp.zeros_like(o_ref)
        x1 = x1_ref[...]
        h = x1 * lax.rsqrt(jnp.mean(x1 * x1, axis=-1, keepdims=True) + EPS) * pre_ref[...]
        h_ref[...] = h.astype(h_ref.dtype)

    h = h_ref[...]
    g = jnp.dot(h, wg_ref[...], preferred_element_type=F32)
    u = jnp.dot(h, wu_ref[...], preferred_element_type=F32)
    act = (_silu(g) * u).astype(BF16)
    o_ref[...] += jnp.dot(act, wd_ref[...], preferred_element_type=F32)

    @pl.when(f == n_steps - 1)
    def _():
        y = o_ref[...]
        o_ref[...] = x1_ref[...] + y * lax.rsqrt(jnp.mean(y * y, axis=-1, keepdims=True) + EPS) * post_ref[...]


def _ffn(x1, pre_w, wg, wu, wd, post_w):
    n_tok = x1.shape[0]
    tm, tf = 512, 256
    n_steps = D_FF // tf
    kern = functools.partial(_ffn_kernel, n_steps=n_steps)
    return pl.pallas_call(
        kern,
        grid=(n_tok // tm, n_steps),
        in_specs=[pl.BlockSpec((tm, D_MODEL), lambda m, f: (m, 0)),
                  pl.BlockSpec((1, D_MODEL), lambda m, f: (0, 0)),
                  pl.BlockSpec((D_MODEL, tf), lambda m, f: (0, f)),
                  pl.BlockSpec((D_MODEL, tf), lambda m, f: (0, f)),
                  pl.BlockSpec((tf, D_MODEL), lambda m, f: (f, 0)),
                  pl.BlockSpec((1, D_MODEL), lambda m, f: (0, 0))],
        out_specs=pl.BlockSpec((tm, D_MODEL), lambda m, f: (m, 0)),
        out_shape=jax.ShapeDtypeStruct((n_tok, D_MODEL), F32),
        scratch_shapes=[pltpu.VMEM((tm, D_MODEL), BF16)],
        compiler_params=_params(2),
        name="swiglu_ffn",
    )(x1, pre_w, wg, wu, wd, post_w)


def kernel(x, positions, mix_pre_norm, w_in, conv_w, conv_b, dt_bias, a_log, d_skip, ssm_norm,
           w_out, mix_post_norm, ffn_pre_norm, w_gate, w_up, w_down, ffn_post_norm):
    batch, seq, _ = x.shape
    depth = w_in.shape[0]
    x2d = x.reshape(batch * seq, D_MODEL)
    cos, sin = _rope_tables(positions)
    for layer in range(depth):
        w_in_bf = w_in[layer].astype(BF16)
        wdt_t = w_in_bf[:, DT_OFF:].T
        proj, dt_t = _in_proj(x2d, mix_pre_norm[layer][None, :], w_in_bf, wdt_t)
        att = _attention(proj, cos, sin, batch, seq)
        ssm = _ssm(proj, dt_t, conv_w[layer], conv_b[layer][None, :], dt_bias[layer], a_log[layer],
                   d_skip[layer], ssm_norm[layer][None, :], batch, seq)
        x1 = _out_proj(att, ssm, w_out[layer].astype(BF16), x2d, mix_post_norm[layer][None, :])
        x2d = _ffn(x1, ffn_pre_norm[layer][None, :], w_gate[layer].astype(BF16), w_up[layer].astype(BF16),
                   w_down[layer].astype(BF16), ffn_post_norm[layer][None, :])
    return x2d.reshape(batch, seq, D_MODEL)
```

```python
import functools
import math

import jax
import jax.numpy as jnp
from jax import lax
from jax.experimental import pallas as pl
from jax.experimental.pallas import tpu as pltpu

F32 = jnp.float32
BF16 = jnp.bfloat16

D_MODEL = 4096
N_HEADS = 16
N_KV_HEADS = 4
KV_GROUP = N_HEADS // N_KV_HEADS
HEAD_DIM = 128
MOBA_BLOCK = 256
MOBA_TOPK = 3
ROPE_THETA = 10000.0
ATT_WIDTH = N_HEADS * HEAD_DIM
KV_DIM = N_KV_HEADS * HEAD_DIM
SSM_WIDTH = 2048
SSM_HEAD_DIM = 64
SSM_HEADS = SSM_WIDTH // SSM_HEAD_DIM
SSM_GROUPS = 4
HEADS_PER_GROUP = SSM_HEADS // SSM_GROUPS
GROUP_WIDTH = SSM_WIDTH // SSM_GROUPS
SSM_STATE = 128
CONV_WIDTH = 4
D_FF = 11008
EPS = 1e-6

Q_OFF = 0
K_OFF = ATT_WIDTH
V_OFF = K_OFF + KV_DIM
Z_OFF = V_OFF + KV_DIM
XS_OFF = Z_OFF + SSM_WIDTH
BM_OFF = XS_OFF + SSM_WIDTH
CM_OFF = BM_OFF + SSM_GROUPS * SSM_STATE
DT_OFF = CM_OFF + SSM_GROUPS * SSM_STATE
PROJ_COLS = DT_OFF

V7X_LANES = 128
V7X_SUBLANES = 8
V7X_VMEM_LIMIT_BYTES = 60000 * 1024

MASK_BIAS = -1e30


def _params(n_axes, vmem_bytes=V7X_VMEM_LIMIT_BYTES):
    return pltpu.CompilerParams(dimension_semantics=("arbitrary",) * n_axes,
                                vmem_limit_bytes=vmem_bytes)


def _sigmoid(x):
    return 1.0 / (1.0 + jnp.exp(-x))


def _silu(x):
    return x * _sigmoid(x)


def _silu_tanh(x):
    h = 0.5 * x
    return h + h * jnp.tanh(h)


def _split3_f32(x):
    hi = x.astype(BF16).astype(F32)
    r1 = x - hi
    mid = r1.astype(BF16).astype(F32)
    lo = (r1 - mid).astype(BF16).astype(F32)
    return hi, mid, lo


def _rope_kernel(pos_ref, invf_ref, cos_ref, sin_ref):
    ang = pos_ref[...].astype(F32) * invf_ref[...]
    lane = lax.broadcasted_iota(jnp.int32, ang.shape, 1)
    cos_ref[...] = jnp.cos(ang)
    sin_ref[...] = jnp.where(lane < HEAD_DIM // 2, -jnp.sin(ang), jnp.sin(ang))


def _rope_tables(positions):
    n = positions.size
    half = HEAD_DIM // 2
    inv_freq = ROPE_THETA ** (-jnp.arange(half, dtype=F32) / half)
    invf = jnp.concatenate([inv_freq, inv_freq])[None, :]
    pos = positions.reshape(n, 1)
    tm = 1024
    return pl.pallas_call(
        _rope_kernel,
        grid=(n // tm,),
        in_specs=[pl.BlockSpec((tm, 1), lambda i: (i, 0)),
                  pl.BlockSpec((1, HEAD_DIM), lambda i: (0, 0))],
        out_specs=[pl.BlockSpec((tm, HEAD_DIM), lambda i: (i, 0)),
                   pl.BlockSpec((tm, HEAD_DIM), lambda i: (i, 0))],
        out_shape=[jax.ShapeDtypeStruct((n, HEAD_DIM), F32)] * 2,
        compiler_params=_params(1),
        name="rope_tables",
    )(pos, invf)


def _rope(t, cos, sin_signed):
    return t * cos + pltpu.roll(t, HEAD_DIM // 2, 1) * sin_signed


def _inproj_kernel(x_ref, nw_ref, w_ref, wdt_ref, proj_ref, dtT_ref, h_ref):
    @pl.when(pl.program_id(1) == 0)
    def _():
        x = x_ref[...]
        h = x * lax.rsqrt(jnp.mean(x * x, axis=-1, keepdims=True) + EPS) * nw_ref[...]
        hb = h.astype(BF16)
        h_ref[...] = hb
        dtT_ref[...] = lax.dot_general(wdt_ref[...], hb, (((1,), (1,)), ((), ())),
                                       preferred_element_type=F32)

    proj_ref[...] = jnp.dot(h_ref[...], w_ref[...], preferred_element_type=F32)


def _in_proj(x2d, norm_w, w_in_bf16, wdt_t):
    n_tok = x2d.shape[0]
    tm, tn = 512, 1024
    return pl.pallas_call(
        _inproj_kernel,
        grid=(n_tok // tm, PROJ_COLS // tn),
        in_specs=[pl.BlockSpec((tm, D_MODEL), lambda m, n: (m, 0)),
                  pl.BlockSpec((1, D_MODEL), lambda m, n: (0, 0)),
                  pl.BlockSpec((D_MODEL, tn), lambda m, n: (0, n)),
                  pl.BlockSpec((SSM_HEADS, D_MODEL), lambda m, n: (0, 0))],
        out_specs=[pl.BlockSpec((tm, tn), lambda m, n: (m, n)),
                   pl.BlockSpec((SSM_HEADS, tm), lambda m, n: (0, m))],
        out_shape=[jax.ShapeDtypeStruct((n_tok, PROJ_COLS), F32),
                   jax.ShapeDtypeStruct((SSM_HEADS, n_tok), F32)],
        scratch_shapes=[pltpu.VMEM((tm, D_MODEL), BF16)],
        compiler_params=_params(2),
        name="in_proj",
    )(x2d, norm_w, w_in_bf16, wdt_t)


def _attn_block_step(n_visible, q_aug, kr_ref, vt_ref, kg_ref, o_ref):
    nt = (((1,), (1,)), ((), ()))
    gate_rows = 2 * V7X_SUBLANES
    n_past = n_visible - 1
    n_keys = n_visible * MOBA_BLOCK

    lhs = jnp.concatenate([kr_ref[0:n_keys, :], kg_ref[...]], axis=0)
    s_all = lax.dot_general(lhs, q_aug, nt, preferred_element_type=F32)
    gate_t = s_all[n_keys:n_keys + V7X_SUBLANES] + s_all[n_keys + gate_rows:n_keys + gate_rows + V7X_SUBLANES]

    blk = lax.broadcasted_iota(jnp.int32, gate_t.shape, 0)
    past = blk < n_past
    gate_t = jnp.where(past, gate_t, -jnp.inf)
    rank = jnp.zeros(gate_t.shape, F32)
    for jp in range(n_past):
        other = gate_t[jp:jp + 1, :]
        rank = rank + jnp.where(other > gate_t, 1.0, jnp.where((other == gate_t) & (jp < blk), 1.0, 0.0))
    bias_t = jnp.where(past & (rank < MOBA_TOPK), 0.0, MASK_BIAS)

    own = s_all[n_past * MOBA_BLOCK:n_keys]
    kpos = lax.broadcasted_iota(jnp.int32, own.shape, 0)
    qpos = lax.broadcasted_iota(jnp.int32, own.shape, 1) & (MOBA_BLOCK - 1)
    blocks = [s_all[j * MOBA_BLOCK:(j + 1) * MOBA_BLOCK] for j in range(n_past)]
    blocks.append(jnp.where(kpos <= qpos, own, -jnp.inf))
    biases = [bias_t[j:j + 1, :] for j in range(n_past)] + [jnp.zeros((1, own.shape[1]), F32)]

    m = jnp.max(blocks[0], axis=0, keepdims=True) + biases[0]
    for blk_s, b in zip(blocks[1:], biases[1:]):
        m = jnp.maximum(m, jnp.max(blk_s, axis=0, keepdims=True) + b)
    probs = [jnp.exp2(blk_s - (m - b)) for blk_s, b in zip(blocks, biases)]
    l = jnp.sum(probs[0], axis=0, keepdims=True)
    for pj in probs[1:]:
        l = l + jnp.sum(pj, axis=0, keepdims=True)
    p_all = jnp.concatenate([pj.astype(BF16) for pj in probs], axis=0)
    acc = jnp.dot(vt_ref[:, 0:n_keys], p_all, preferred_element_type=F32)
    out = (acc * (1.0 / l)).T
    for g in range(KV_GROUP):
        o_ref[:, g * HEAD_DIM:(g + 1) * HEAD_DIM] = out[g * MOBA_BLOCK:(g + 1) * MOBA_BLOCK].astype(o_ref.dtype)


def _attn_kernel(q_ref, k_ref, v_ref, cq_ref, sq_ref, ck_ref, sk_ref, o_ref,
                 kr_ref, vt_ref, kg_ref, *, n_blocks):
    i = pl.program_id(2)
    scale = HEAD_DIM ** -0.5 * math.log2(math.e)
    gate_rows = 2 * V7X_SUBLANES

    @pl.when(i == 0)
    def _():
        kr = _rope(k_ref[...], ck_ref[...], sk_ref[...])
        kr_ref[:, :HEAD_DIM] = kr.astype(BF16)
        kr_ref[:, HEAD_DIM:] = jnp.zeros(kr.shape, BF16)
        vt_ref[...] = v_ref[...].T.astype(BF16)
        km = jnp.mean(kr.reshape(n_blocks, MOBA_BLOCK, HEAD_DIM), axis=1)
        km_hi, km_mid, _ = _split3_f32(km)
        zpad = jnp.zeros((gate_rows - n_blocks, 2 * HEAD_DIM), F32)
        kg_ref[...] = jnp.concatenate(
            [jnp.concatenate([km_hi, km_hi], axis=1), zpad,
             jnp.concatenate([km_mid, jnp.zeros_like(km_mid)], axis=1), zpad], axis=0).astype(BF16)

    cq = cq_ref[...]
    sq = sq_ref[...]
    q_all = jnp.concatenate(
        [_rope(q_ref[:, g * HEAD_DIM:(g + 1) * HEAD_DIM], cq, sq) * scale for g in range(KV_GROUP)],
        axis=0)
    q_hi = q_all.astype(BF16)
    q_mid = (q_all - q_hi.astype(F32)).astype(BF16)
    q_aug = jnp.concatenate([q_hi, q_mid], axis=1)

    for n_visible in range(1, n_blocks + 1):
        pl.when(i == n_visible - 1)(
            functools.partial(_attn_block_step, n_visible, q_aug, kr_ref, vt_ref, kg_ref, o_ref))


def _attention(proj, cos, sin, batch, seq):
    n_blocks = seq // MOBA_BLOCK
    assert n_blocks <= V7X_SUBLANES, "block gate layout holds at most 8 key blocks"
    qw = KV_GROUP * HEAD_DIM
    kern = functools.partial(_attn_kernel, n_blocks=n_blocks)
    return pl.pallas_call(
        kern,
        grid=(batch, N_KV_HEADS, n_blocks),
        in_specs=[
            pl.BlockSpec((MOBA_BLOCK, qw), lambda b, h, i: (b * n_blocks + i, h)),
            pl.BlockSpec((seq, HEAD_DIM), lambda b, h, i: (b, K_OFF // HEAD_DIM + h)),
            pl.BlockSpec((seq, HEAD_DIM), lambda b, h, i: (b, V_OFF // HEAD_DIM + h)),
            pl.BlockSpec((MOBA_BLOCK, HEAD_DIM), lambda b, h, i: (b * n_blocks + i, 0)),
            pl.BlockSpec((MOBA_BLOCK, HEAD_DIM), lambda b, h, i: (b * n_blocks + i, 0)),
            pl.BlockSpec((seq, HEAD_DIM), lambda b, h, i: (b, 0)),
            pl.BlockSpec((seq, HEAD_DIM), lambda b, h, i: (b, 0)),
        ],
        out_specs=pl.BlockSpec((MOBA_BLOCK, qw), lambda b, h, i: (b * n_blocks + i, h)),
        out_shape=jax.ShapeDtypeStruct((batch * seq, ATT_WIDTH), BF16),
        scratch_shapes=[pltpu.VMEM((seq, 2 * HEAD_DIM), BF16),
                        pltpu.VMEM((HEAD_DIM, seq), BF16),
                        pltpu.VMEM((4 * V7X_SUBLANES, 2 * HEAD_DIM), BF16)],
        compiler_params=_params(3),
        name="moba_attention",
    )(proj, proj, proj, cos, sin, cos, sin)


SSD_ROWS = 512
SSD_SUB = 128
N_EXPAND = 3


def _ssd_constants():
    hg = HEADS_PER_GROUP
    n_rows = N_EXPAND * 3 * hg
    r = jnp.arange(n_rows + (-n_rows) % (2 * V7X_SUBLANES))[:, None]
    c = jnp.arange(N_EXPAND * GROUP_WIDTH)[None, :]
    e = (r < n_rows) & (r // (3 * hg) == c // GROUP_WIDTH) & (r % hg == (c % GROUP_WIDTH) // SSM_HEAD_DIM)
    r2 = jnp.arange(6 * hg)[:, None]
    c2 = jnp.arange(hg * SSD_SUB)[None, :]
    d = (r2 % hg) == (c2 // SSD_SUB)
    return e.astype(BF16), d.astype(F32)


def _ssm_kernel(z_ref, xs_ref, bm_ref, cm_ref, dtT_ref,
                cwx_ref, cwb_ref, cwc_ref, cbx_ref, cbb_ref, cbc_ref,
                dtb_ref, alog_ref, dsk_ref, nw_ref, esel_ref, dsel_ref, o_ref,
                xt_ref, bt_ref, ct_ref, st_ref):
    c = pl.program_id(2)
    R = SSD_ROWS
    T = SSD_SUB
    hg = HEADS_PER_GROUP
    tail = V7X_SUBLANES
    tn = (((0,), (0,)), ((), ()))
    nt = (((1,), (1,)), ((), ()))

    @pl.when(c == 0)
    def _():
        for ext_ref in (xt_ref, bt_ref, ct_ref):
            ext_ref[0:tail, :] = jnp.zeros((tail, ext_ref.shape[1]), F32)
        st_ref[...] = jnp.zeros_like(st_ref)

    def conv_silu(u_ref, ext_ref, w_ref, b_ref):
        ext_ref[tail:tail + R, :] = u_ref[...]
        acc = b_ref[...]
        for k in range(CONV_WIDTH):
            lo = tail - (CONV_WIDTH - 1) + k
            acc = acc + w_ref[k:k + 1, :] * ext_ref[lo:lo + R, :]
        ext_ref[0:tail, :] = ext_ref[R:R + tail, :]
        return _silu_tanh(acc)

    xs = conv_silu(xs_ref, xt_ref, cwx_ref, cbx_ref)
    bm_bf = conv_silu(bm_ref, bt_ref, cwb_ref, cbb_ref).astype(BF16)
    cm_bf = conv_silu(cm_ref, ct_ref, cwc_ref, cbc_ref).astype(BF16)

    dt_in = dtT_ref[...] + dtb_ref[...]
    dt_t = jnp.maximum(dt_in, 0.0) + jnp.log1p(jnp.exp(-jnp.abs(dt_in)))
    adt_t = dt_t * (-jnp.exp(alog_ref[...]))

    lane = lax.broadcasted_iota(jnp.int32, (hg, T), 1)
    li = lax.broadcasted_iota(jnp.int32, (T, T), 0)
    si = lax.broadcasted_iota(jnp.int32, (T, T), 1)
    causal = si <= li
    half = lax.broadcasted_iota(jnp.int32, (T, V7X_LANES), 1) < SSM_HEAD_DIM
    ones = jnp.ones((hg, T), F32)
    esel = esel_ref[...]
    dsel = dsel_ref[...]

    st = st_ref[...]
    y_rows = []
    for sc in range(R // T):
        r0 = sc * T
        acum = adt_t[:, r0:r0 + T]
        shift = 1
        while shift < T:
            acum = acum + jnp.where(lane >= shift, pltpu.roll(acum, shift, 1), 0.0)
            shift *= 2
        dt_c = dt_t[:, r0:r0 + T]
        a_last = acum[:, T - 1:T]
        ecum = jnp.exp(acum)
        w = dt_c * jnp.exp(a_last - acum)

        parts = []
        for v in (dt_c, w, ecum):
            parts.extend(_split3_f32(v))
        pad_rows = esel.shape[0] - len(parts) * hg
        if pad_rows:
            parts.append(jnp.zeros((pad_rows, T), F32))
        stacked = jnp.concatenate(parts, axis=0).astype(BF16)
        expd = lax.dot_general(stacked, esel, tn, preferred_element_type=F32)
        dt_exp = expd[:, :GROUP_WIDTH]
        w_exp = expd[:, GROUP_WIDTH:2 * GROUP_WIDTH]
        ecum_exp = expd[:, 2 * GROUP_WIDTH:]

        a_hi, a_mid, a_lo = _split3_f32(acum * math.log2(math.e))
        lhs_t = jnp.concatenate([a_hi, a_mid, a_lo, ones, ones, ones], axis=0).astype(BF16)
        rhs = jnp.concatenate([ones, ones, ones, -a_hi, -a_mid, -a_lo], axis=0)
        rhs = (jnp.concatenate([rhs] * hg, axis=1) * dsel).astype(BF16)
        seg = lax.dot_general(lhs_t, rhs, tn, preferred_element_type=F32)

        xs_c = xs[r0:r0 + T]
        x_dt_bf = (xs_c * dt_exp).astype(BF16)
        bm_c = bm_bf[r0:r0 + T]
        cm_c = cm_bf[r0:r0 + T]
        cb = lax.dot_general(cm_c, bm_c, nt, preferred_element_type=F32)

        y_pairs = []
        for pr in range(hg // 2):
            mix = []
            for h in (2 * pr, 2 * pr + 1):
                dec = jnp.exp2(jnp.where(causal, seg[:, h * T:(h + 1) * T], -jnp.inf))
                mix.append((cb * dec).astype(BF16))
            x_pair = x_dt_bf[:, pr * V7X_LANES:(pr + 1) * V7X_LANES]
            zero = jnp.zeros_like(x_pair)
            x2 = jnp.concatenate([jnp.where(half, x_pair, zero), jnp.where(half, zero, x_pair)], axis=0)
            y_pairs.append(jnp.dot(jnp.concatenate(mix, axis=1), x2, preferred_element_type=F32))
        y_diag = jnp.concatenate(y_pairs, axis=1)

        y_off = jnp.dot(cm_c, st.astype(BF16), preferred_element_type=F32) * ecum_exp
        xw = (xs_c * w_exp).astype(BF16)
        states_t = lax.dot_general(bm_c, xw, tn, preferred_element_type=F32)
        st = st * ecum_exp[T - 1:T, :] + states_t
        y_rows.append(y_diag + y_off)

    st_ref[...] = st
    y = jnp.concatenate(y_rows, axis=0) + xs * dsk_ref[...]
    y = y * _silu_tanh(z_ref[...])
    y = y * lax.rsqrt(jnp.mean(y * y, axis=-1, keepdims=True) + EPS) * nw_ref[...]
    o_ref[...] = y.astype(o_ref.dtype)


def _ssm(proj, dt_t, conv_w, conv_b, dt_bias, a_log, d_skip, ssm_norm, batch, seq):
    R = SSD_ROWS
    nc = seq // R
    gw = GROUP_WIDTH
    hg = HEADS_PER_GROUP
    dsk = jnp.repeat(d_skip, SSM_HEAD_DIM)[None, :]
    esel, dsel = _ssd_constants()
    row = lambda b, g, c: b * nc + c
    in_specs = [
        pl.BlockSpec((R, gw), lambda b, g, c: (row(b, g, c), Z_OFF // gw + g)),
        pl.BlockSpec((R, gw), lambda b, g, c: (row(b, g, c), XS_OFF // gw + g)),
        pl.BlockSpec((R, SSM_STATE), lambda b, g, c: (row(b, g, c), BM_OFF // SSM_STATE + g)),
        pl.BlockSpec((R, SSM_STATE), lambda b, g, c: (row(b, g, c), CM_OFF // SSM_STATE + g)),
        pl.BlockSpec((hg, R), lambda b, g, c: (g, row(b, g, c))),
        pl.BlockSpec((CONV_WIDTH, gw), lambda b, g, c: (0, g)),
        pl.BlockSpec((CONV_WIDTH, SSM_STATE), lambda b, g, c: (0, SSM_WIDTH // SSM_STATE + g)),
        pl.BlockSpec((CONV_WIDTH, SSM_STATE), lambda b, g, c: (0, SSM_WIDTH // SSM_STATE + SSM_GROUPS + g)),
        pl.BlockSpec((1, gw), lambda b, g, c: (0, g)),
        pl.BlockSpec((1, SSM_STATE), lambda b, g, c: (0, SSM_WIDTH // SSM_STATE + g)),
        pl.BlockSpec((1, SSM_STATE), lambda b, g, c: (0, SSM_WIDTH // SSM_STATE + SSM_GROUPS + g)),
        pl.BlockSpec((hg, 1), lambda b, g, c: (g, 0)),
        pl.BlockSpec((hg, 1), lambda b, g, c: (g, 0)),
        pl.BlockSpec((1, gw), lambda b, g, c: (0, g)),
        pl.BlockSpec((1, gw), lambda b, g, c: (0, g)),
        pl.BlockSpec(esel.shape, lambda b, g, c: (0, 0)),
        pl.BlockSpec(dsel.shape, lambda b, g, c: (0, 0)),
    ]
    return pl.pallas_call(
        _ssm_kernel,
        grid=(batch, SSM_GROUPS, nc),
        in_specs=in_specs,
        out_specs=pl.BlockSpec((R, gw), lambda b, g, c: (row(b, g, c), g)),
        out_shape=jax.ShapeDtypeStruct((batch * seq, SSM_WIDTH), BF16),
        scratch_shapes=[pltpu.VMEM((V7X_SUBLANES + R, gw), F32),
                        pltpu.VMEM((V7X_SUBLANES + R, SSM_STATE), F32),
                        pltpu.VMEM((V7X_SUBLANES + R, SSM_STATE), F32),
                        pltpu.VMEM((SSM_STATE, gw), F32)],
        compiler_params=_params(3),
        name="ssd_mixer",
    )(proj, proj, proj, proj, dt_t,
      conv_w, conv_w, conv_w, conv_b, conv_b, conv_b,
      dt_bias.reshape(SSM_HEADS, 1), a_log.reshape(SSM_HEADS, 1), dsk, ssm_norm, esel, dsel)


def _outproj_kernel(att_ref, ssm_ref, wa_ref, ws_ref, x_ref, post_ref, x1_ref, *, tn, n_steps):
    n = pl.program_id(1)
    mixed = (jnp.dot(att_ref[...], wa_ref[...], preferred_element_type=F32)
             + jnp.dot(ssm_ref[...], ws_ref[...], preferred_element_type=F32))
    x1_ref[:, pl.ds(pl.multiple_of(n * tn, tn), tn)] = mixed

    @pl.when(n == n_steps - 1)
    def _():
        y = x1_ref[...]
        x1_ref[...] = x_ref[...] + y * lax.rsqrt(jnp.mean(y * y, axis=-1, keepdims=True) + EPS) * post_ref[...]


def _out_proj(att, ssm, w_out_bf16, x2d, post_w):
    n_tok = x2d.shape[0]
    tm, tn = 512, 512
    n_steps = D_MODEL // tn
    kern = functools.partial(_outproj_kernel, tn=tn, n_steps=n_steps)
    return pl.pallas_call(
        kern,
        grid=(n_tok // tm, n_steps),
        in_specs=[pl.BlockSpec((tm, ATT_WIDTH), lambda m, n: (m, 0)),
                  pl.BlockSpec((tm, SSM_WIDTH), lambda m, n: (m, 0)),
                  pl.BlockSpec((ATT_WIDTH, tn), lambda m, n: (0, n)),
                  pl.BlockSpec((SSM_WIDTH, tn), lambda m, n: (1, n)),
                  pl.BlockSpec((tm, D_MODEL), lambda m, n: (m, 0)),
                  pl.BlockSpec((1, D_MODEL), lambda m, n: (0, 0))],
        out_specs=pl.BlockSpec((tm, D_MODEL), lambda m, n: (m, 0)),
        out_shape=jax.ShapeDtypeStruct((n_tok, D_MODEL), F32),
        compiler_params=_params(2),
        name="out_proj",
    )(att, ssm, w_out_bf16, w_out_bf16, x2d, post_w)


def _ffn_kernel(x1_ref, pre_ref, wg_ref, wu_ref, wd_ref, post_ref, o_ref, h_ref, *, n_steps):
    f = pl.program_id(1)

    @pl.when(f == 0)
    def _():
        o_ref[...] = jnp.zeros_like(o_ref)
        x1 = x1_ref[...]
        h = x1 * lax.rsqrt(jnp.mean(x1 * x1, axis=-1, keepdims=True) + EPS) * pre_ref[...]
        h_ref[...] = h.astype(h_ref.dtype)

    h = h_ref[...]
    g = jnp.dot(h, wg_ref[...], preferred_element_type=F32)
    u = jnp.dot(h, wu_ref[...], preferred_element_type=F32)
    act = (_silu(g) * u).astype(BF16)
    o_ref[...] += jnp.dot(act, wd_ref[...], preferred_element_type=F32)

    @pl.when(f == n_steps - 1)
    def _():
        y = o_ref[...]
        o_ref[...] = x1_ref[...] + y * lax.rsqrt(jnp.mean(y * y, axis=-1, keepdims=True) + EPS) * post_ref[...]


def _ffn(x1, pre_w, wg, wu, wd, post_w):
    n_tok = x1.shape[0]
    tm, tf = 512, 256
    n_steps = D_FF // tf
    kern = functools.partial(_ffn_kernel, n_steps=n_steps)
    return pl.pallas_call(
        kern,
        grid=(n_tok // tm, n_steps),
        in_specs=[pl.BlockSpec((tm, D_MODEL), lambda m, f: (m, 0)),
                  pl.BlockSpec((1, D_MODEL), lambda m, f: (0, 0)),
                  pl.BlockSpec((D_MODEL, tf), lambda m, f: (0, f)),
                  pl.BlockSpec((D_MODEL, tf), lambda m, f: (0, f)),
                  pl.BlockSpec((tf, D_MODEL), lambda m, f: (f, 0)),
                  pl.BlockSpec((1, D_MODEL), lambda m, f: (0, 0))],
        out_specs=pl.BlockSpec((tm, D_MODEL), lambda m, f: (m, 0)),
        out_shape=jax.ShapeDtypeStruct((n_tok, D_MODEL), F32),
        scratch_shapes=[pltpu.VMEM((tm, D_MODEL), BF16)],
        compiler_params=_params(2),
        name="swiglu_ffn",
    )(x1, pre_w, wg, wu, wd, post_w)


def kernel(x, positions, mix_pre_norm, w_in, conv_w, conv_b, dt_bias, a_log, d_skip, ssm_norm,
           w_out, mix_post_norm, ffn_pre_norm, w_gate, w_up, w_down, ffn_post_norm):
    batch, seq, _ = x.shape
    depth = w_in.shape[0]
    x2d = x.reshape(batch * seq, D_MODEL)
    cos, sin = _rope_tables(positions)
    for layer in range(depth):
        w_in_bf = w_in[layer].astype(BF16)
        wdt_t = w_in[layer][:, DT_OFF:].T.astype(BF16)
        proj, dt_t = _in_proj(x2d, mix_pre_norm[layer][None, :], w_in_bf, wdt_t)
        att = _attention(proj, cos, sin, batch, seq)
        ssm = _ssm(proj, dt_t, conv_w[layer], conv_b[layer][None, :], dt_bias[layer], a_log[layer],
                   d_skip[layer], ssm_norm[layer][None, :], batch, seq)
        x1 = _out_proj(att, ssm, w_out[layer].astype(BF16), x2d, mix_post_norm[layer][None, :])
        x2d = _ffn(x1, ffn_pre_norm[layer][None, :], w_gate[layer].astype(BF16), w_up[layer].astype(BF16),
                   w_down[layer].astype(BF16), ffn_post_norm[layer][None, :])
    return x2d.reshape(batch, seq, D_MODEL)
```

```python
import functools
import math

import jax
import jax.numpy as jnp
from jax import lax
from jax.experimental import pallas as pl
from jax.experimental.pallas import tpu as pltpu

F32 = jnp.float32
BF16 = jnp.bfloat16

D_MODEL = 4096
N_HEADS = 16
N_KV_HEADS = 4
KV_GROUP = N_HEADS // N_KV_HEADS
HEAD_DIM = 128
MOBA_BLOCK = 256
MOBA_TOPK = 3
ROPE_THETA = 10000.0
ATT_WIDTH = N_HEADS * HEAD_DIM
KV_DIM = N_KV_HEADS * HEAD_DIM
SSM_WIDTH = 2048
SSM_HEAD_DIM = 64
SSM_HEADS = SSM_WIDTH // SSM_HEAD_DIM
SSM_GROUPS = 4
HEADS_PER_GROUP = SSM_HEADS // SSM_GROUPS
GROUP_WIDTH = SSM_WIDTH // SSM_GROUPS
SSM_STATE = 128
CONV_WIDTH = 4
D_FF = 11008
EPS = 1e-6

Q_OFF = 0
K_OFF = ATT_WIDTH
V_OFF = K_OFF + KV_DIM
Z_OFF = V_OFF + KV_DIM
XS_OFF = Z_OFF + SSM_WIDTH
BM_OFF = XS_OFF + SSM_WIDTH
CM_OFF = BM_OFF + SSM_GROUPS * SSM_STATE
DT_OFF = CM_OFF + SSM_GROUPS * SSM_STATE
PROJ_COLS = DT_OFF

V7X_LANES = 128
V7X_SUBLANES = 8
V7X_VMEM_LIMIT_BYTES = 60000 * 1024

MASK_BIAS = -1e30


def _params(n_axes, vmem_bytes=V7X_VMEM_LIMIT_BYTES):
    return pltpu.CompilerParams(dimension_semantics=("arbitrary",) * n_axes,
                                vmem_limit_bytes=vmem_bytes)


def _sigmoid(x):
    return 1.0 / (1.0 + jnp.exp(-x))


def _silu(x):
    return x * _sigmoid(x)


def _silu_tanh(x):
    h = 0.5 * x
    return h + h * jnp.tanh(h)


def _split3_f32(x):
    hi = x.astype(BF16).astype(F32)
    r1 = x - hi
    mid = r1.astype(BF16).astype(F32)
    lo = (r1 - mid).astype(BF16).astype(F32)
    return hi, mid, lo


def _rope_kernel(pos_ref, invf_ref, cos_ref, sin_ref):
    ang = pos_ref[...].astype(F32) * invf_ref[...]
    lane = lax.broadcasted_iota(jnp.int32, ang.shape, 1)
    cos_ref[...] = jnp.cos(ang)
    sin_ref[...] = jnp.where(lane < HEAD_DIM // 2, -jnp.sin(ang), jnp.sin(ang))


def _rope_tables(positions):
    n = positions.size
    half = HEAD_DIM // 2
    inv_freq = ROPE_THETA ** (-jnp.arange(half, dtype=F32) / half)
    invf = jnp.concatenate([inv_freq, inv_freq])[None, :]
    pos = positions.reshape(n, 1)
    tm = 1024
    return pl.pallas_call(
        _rope_kernel,
        grid=(n // tm,),
        in_specs=[pl.BlockSpec((tm, 1), lambda i: (i, 0)),
                  pl.BlockSpec((1, HEAD_DIM), lambda i: (0, 0))],
        out_specs=[pl.BlockSpec((tm, HEAD_DIM), lambda i: (i, 0)),
                   pl.BlockSpec((tm, HEAD_DIM), lambda i: (i, 0))],
        out_shape=[jax.ShapeDtypeStruct((n, HEAD_DIM), F32)] * 2,
        compiler_params=_params(1),
        name="rope_tables",
    )(pos, invf)


def _rope(t, cos, sin_signed):
    return t * cos + pltpu.roll(t, HEAD_DIM // 2, 1) * sin_signed


def _inproj_kernel(x_ref, nw_ref, w_ref, wdt_ref, proj_ref, dtT_ref, h_ref):
    @pl.when(pl.program_id(1) == 0)
    def _():
        x = x_ref[...]
        h = x * lax.rsqrt(jnp.mean(x * x, axis=-1, keepdims=True) + EPS) * nw_ref[...]
        hb = h.astype(BF16)
        h_ref[...] = hb
        dtT_ref[...] = lax.dot_general(wdt_ref[...], hb, (((1,), (1,)), ((), ())),
                                       preferred_element_type=F32)

    proj_ref[...] = lax.dot_general(h_ref[...], w_ref[...], (((1,), (1,)), ((), ())),
                                    preferred_element_type=F32)


def _in_proj(x2d, norm_w, w_in_t):
    n_tok = x2d.shape[0]
    tm, tn = 512, 1024
    return pl.pallas_call(
        _inproj_kernel,
        grid=(n_tok // tm, PROJ_COLS // tn),
        in_specs=[pl.BlockSpec((tm, D_MODEL), lambda m, n: (m, 0)),
                  pl.BlockSpec((1, D_MODEL), lambda m, n: (0, 0)),
                  pl.BlockSpec((tn, D_MODEL), lambda m, n: (n, 0)),
                  pl.BlockSpec((SSM_HEADS, D_MODEL), lambda m, n: (DT_OFF // SSM_HEADS, 0))],
        out_specs=[pl.BlockSpec((tm, tn), lambda m, n: (m, n)),
                   pl.BlockSpec((SSM_HEADS, tm), lambda m, n: (0, m))],
        out_shape=[jax.ShapeDtypeStruct((n_tok, PROJ_COLS), F32),
                   jax.ShapeDtypeStruct((SSM_HEADS, n_tok), F32)],
        scratch_shapes=[pltpu.VMEM((tm, D_MODEL), BF16)],
        compiler_params=_params(2),
        name="in_proj",
    )(x2d, norm_w, w_in_t, w_in_t)


def _attn_block_step(n_visible, q_aug, kr_ref, vt_ref, kg_ref, o_ref):
    nt = (((1,), (1,)), ((), ()))
    gate_rows = 2 * V7X_SUBLANES
    n_past = n_visible - 1
    n_keys = n_visible * MOBA_BLOCK

    lhs = jnp.concatenate([kr_ref[0:n_keys, :], kg_ref[...]], axis=0)
    s_all = lax.dot_general(lhs, q_aug, nt, preferred_element_type=F32)
    gate_t = s_all[n_keys:n_keys + V7X_SUBLANES] + s_all[n_keys + gate_rows:n_keys + gate_rows + V7X_SUBLANES]

    blk = lax.broadcasted_iota(jnp.int32, gate_t.shape, 0)
    past = blk < n_past
    gate_t = jnp.where(past, gate_t, -jnp.inf)
    rank = jnp.zeros(gate_t.shape, F32)
    for jp in range(n_past):
        other = gate_t[jp:jp + 1, :]
        rank = rank + jnp.where(other > gate_t, 1.0, jnp.where((other == gate_t) & (jp < blk), 1.0, 0.0))
    bias_t = jnp.where(past & (rank < MOBA_TOPK), 0.0, MASK_BIAS)

    own = s_all[n_past * MOBA_BLOCK:n_keys]
    kpos = lax.broadcasted_iota(jnp.int32, own.shape, 0)
    qpos = lax.broadcasted_iota(jnp.int32, own.shape, 1) & (MOBA_BLOCK - 1)
    blocks = [s_all[j * MOBA_BLOCK:(j + 1) * MOBA_BLOCK] for j in range(n_past)]
    blocks.append(jnp.where(kpos <= qpos, own, -jnp.inf))
    biases = [bias_t[j:j + 1, :] for j in range(n_past)] + [jnp.zeros((1, own.shape[1]), F32)]

    m = jnp.max(blocks[0], axis=0, keepdims=True) + biases[0]
    for blk_s, b in zip(blocks[1:], biases[1:]):
        m = jnp.maximum(m, jnp.max(blk_s, axis=0, keepdims=True) + b)
    probs = [jnp.exp2(blk_s - (m - b)) for blk_s, b in zip(blocks, biases)]
    l = jnp.sum(probs[0], axis=0, keepdims=True)
    for pj in probs[1:]:
        l = l + jnp.sum(pj, axis=0, keepdims=True)
    p_all = jnp.concatenate([pj.astype(BF16) for pj in probs], axis=0)
    acc = jnp.dot(vt_ref[:, 0:n_keys], p_all, preferred_element_type=F32)
    out = (acc * (1.0 / l)).T
    for g in range(KV_GROUP):
        o_ref[:, g * HEAD_DIM:(g + 1) * HEAD_DIM] = out[g * MOBA_BLOCK:(g + 1) * MOBA_BLOCK].astype(o_ref.dtype)


def _attn_kernel(q_ref, k_ref, v_ref, cq_ref, sq_ref, ck_ref, sk_ref, o_ref,
                 kr_ref, vt_ref, kg_ref, *, n_blocks):
    i = pl.program_id(2)
    scale = HEAD_DIM ** -0.5 * math.log2(math.e)
    gate_rows = 2 * V7X_SUBLANES

    @pl.when(i == 0)
    def _():
        kr = _rope(k_ref[...], ck_ref[...], sk_ref[...])
        kr_ref[:, :HEAD_DIM] = kr.astype(BF16)
        kr_ref[:, HEAD_DIM:] = jnp.zeros(kr.shape, BF16)
        vt_ref[...] = v_ref[...].T.astype(BF16)
        km = jnp.mean(kr.reshape(n_blocks, MOBA_BLOCK, HEAD_DIM), axis=1)
        km_hi, km_mid, _ = _split3_f32(km)
        zpad = jnp.zeros((gate_rows - n_blocks, 2 * HEAD_DIM), F32)
        kg_ref[...] = jnp.concatenate(
            [jnp.concatenate([km_hi, km_hi], axis=1), zpad,
             jnp.concatenate([km_mid, jnp.zeros_like(km_mid)], axis=1), zpad], axis=0).astype(BF16)

    cq = cq_ref[...]
    sq = sq_ref[...]
    q_all = jnp.concatenate(
        [_rope(q_ref[:, g * HEAD_DIM:(g + 1) * HEAD_DIM], cq, sq) * scale for g in range(KV_GROUP)],
        axis=0)
    q_hi = q_all.astype(BF16)
    q_mid = (q_all - q_hi.astype(F32)).astype(BF16)
    q_aug = jnp.concatenate([q_hi, q_mid], axis=1)

    for n_visible in range(1, n_blocks + 1):
        pl.when(i == n_visible - 1)(
            functools.partial(_attn_block_step, n_visible, q_aug, kr_ref, vt_ref, kg_ref, o_ref))


def _attention(proj, cos, sin, batch, seq):
    n_blocks = seq // MOBA_BLOCK
    assert n_blocks <= V7X_SUBLANES, "block gate layout holds at most 8 key blocks"
    qw = KV_GROUP * HEAD_DIM
    kern = functools.partial(_attn_kernel, n_blocks=n_blocks)
    return pl.pallas_call(
        kern,
        grid=(batch, N_KV_HEADS, n_blocks),
        in_specs=[
            pl.BlockSpec((MOBA_BLOCK, qw), lambda b, h, i: (b * n_blocks + i, h)),
            pl.BlockSpec((seq, HEAD_DIM), lambda b, h, i: (b, K_OFF // HEAD_DIM + h)),
            pl.BlockSpec((seq, HEAD_DIM), lambda b, h, i: (b, V_OFF // HEAD_DIM + h)),
            pl.BlockSpec((MOBA_BLOCK, HEAD_DIM), lambda b, h, i: (b * n_blocks + i, 0)),
            pl.BlockSpec((MOBA_BLOCK, HEAD_DIM), lambda b, h, i: (b * n_blocks + i, 0)),
            pl.BlockSpec((seq, HEAD_DIM), lambda b, h, i: (b, 0)),
            pl.BlockSpec((seq, HEAD_DIM), lambda b, h, i: (b, 0)),
        ],
        out_specs=pl.BlockSpec((MOBA_BLOCK, qw), lambda b, h, i: (b * n_blocks + i, h)),
        out_shape=jax.ShapeDtypeStruct((batch * seq, ATT_WIDTH), BF16),
        scratch_shapes=[pltpu.VMEM((seq, 2 * HEAD_DIM), BF16),
                        pltpu.VMEM((HEAD_DIM, seq), BF16),
                        pltpu.VMEM((4 * V7X_SUBLANES, 2 * HEAD_DIM), BF16)],
        compiler_params=_params(3),
        name="moba_attention",
    )(proj, proj, proj, cos, sin, cos, sin)


SSD_ROWS = 512
SSD_SUB = 128
N_EXPAND = 3


def _ssd_constants():
    hg = HEADS_PER_GROUP
    n_rows = N_EXPAND * 3 * hg
    r = jnp.arange(n_rows + (-n_rows) % (2 * V7X_SUBLANES))[:, None]
    c = jnp.arange(N_EXPAND * GROUP_WIDTH)[None, :]
    e = (r < n_rows) & (r // (3 * hg) == c // GROUP_WIDTH) & (r % hg == (c % GROUP_WIDTH) // SSM_HEAD_DIM)
    r2 = jnp.arange(6 * hg)[:, None]
    c2 = jnp.arange(hg * SSD_SUB)[None, :]
    d = (r2 % hg) == (c2 // SSD_SUB)
    return e.astype(BF16), d.astype(F32)


def _ssm_kernel(z_ref, xs_ref, bm_ref, cm_ref, dtT_ref,
                cwx_ref, cwb_ref, cwc_ref, cbx_ref, cbb_ref, cbc_ref,
                dtb_ref, alog_ref, dsk_ref, nw_ref, esel_ref, dsel_ref, o_ref,
                xt_ref, bt_ref, ct_ref, st_ref):
    c = pl.program_id(2)
    R = SSD_ROWS
    T = SSD_SUB
    hg = HEADS_PER_GROUP
    tail = V7X_SUBLANES
    tn = (((0,), (0,)), ((), ()))
    nt = (((1,), (1,)), ((), ()))

    @pl.when(c == 0)
    def _():
        for ext_ref in (xt_ref, bt_ref, ct_ref):
            ext_ref[0:tail, :] = jnp.zeros((tail, ext_ref.shape[1]), F32)
        st_ref[...] = jnp.zeros_like(st_ref)

    def conv_silu(u_ref, ext_ref, w_ref, b_ref):
        ext_ref[tail:tail + R, :] = u_ref[...]
        acc = b_ref[...]
        for k in range(CONV_WIDTH):
            lo = tail - (CONV_WIDTH - 1) + k
            acc = acc + w_ref[k:k + 1, :] * ext_ref[lo:lo + R, :]
        ext_ref[0:tail, :] = ext_ref[R:R + tail, :]
        return _silu_tanh(acc)

    xs = conv_silu(xs_ref, xt_ref, cwx_ref, cbx_ref)
    bm_bf = conv_silu(bm_ref, bt_ref, cwb_ref, cbb_ref).astype(BF16)
    cm_bf = conv_silu(cm_ref, ct_ref, cwc_ref, cbc_ref).astype(BF16)

    dt_in = dtT_ref[...] + dtb_ref[...]
    dt_t = jnp.maximum(dt_in, 0.0) + jnp.log1p(jnp.exp(-jnp.abs(dt_in)))
    adt_t = dt_t * (-jnp.exp(alog_ref[...]))

    lane = lax.broadcasted_iota(jnp.int32, (hg, T), 1)
    li = lax.broadcasted_iota(jnp.int32, (T, T), 0)
    si = lax.broadcasted_iota(jnp.int32, (T, T), 1)
    causal = si <= li
    half = lax.broadcasted_iota(jnp.int32, (T, V7X_LANES), 1) < SSM_HEAD_DIM
    ones = jnp.ones((hg, T), F32)
    esel = esel_ref[...]
    dsel = dsel_ref[...]

    st = st_ref[...]
    y_rows = []
    for sc in range(R // T):
        r0 = sc * T
        acum = adt_t[:, r0:r0 + T]
        shift = 1
        while shift < T:
            acum = acum + jnp.where(lane >= shift, pltpu.roll(acum, shift, 1), 0.0)
            shift *= 2
        dt_c = dt_t[:, r0:r0 + T]
        a_last = acum[:, T - 1:T]
        ecum = jnp.exp(acum)
        w = dt_c * jnp.exp(a_last - acum)

        parts = []
        for v in (dt_c, w, ecum):
            parts.extend(_split3_f32(v))
        pad_rows = esel.shape[0] - len(parts) * hg
        if pad_rows:
            parts.append(jnp.zeros((pad_rows, T), F32))
        stacked = jnp.concatenate(parts, axis=0).astype(BF16)
        expd = lax.dot_general(stacked, esel, tn, preferred_element_type=F32)
        dt_exp = expd[:, :GROUP_WIDTH]
        w_exp = expd[:, GROUP_WIDTH:2 * GROUP_WIDTH]
        ecum_exp = expd[:, 2 * GROUP_WIDTH:]

        a_hi, a_mid, a_lo = _split3_f32(acum * math.log2(math.e))
        lhs_t = jnp.concatenate([a_hi, a_mid, a_lo, ones, ones, ones], axis=0).astype(BF16)
        rhs = jnp.concatenate([ones, ones, ones, -a_hi, -a_mid, -a_lo], axis=0)
        rhs = (jnp.concatenate([rhs] * hg, axis=1) * dsel).astype(BF16)
        seg = lax.dot_general(lhs_t, rhs, tn, preferred_element_type=F32)

        xs_c = xs[r0:r0 + T]
        x_dt_bf = (xs_c * dt_exp).astype(BF16)
        bm_c = bm_bf[r0:r0 + T]
        cm_c = cm_bf[r0:r0 + T]
        cb = lax.dot_general(cm_c, bm_c, nt, preferred_element_type=F32)

        y_pairs = []
        for pr in range(hg // 2):
            mix = []
            for h in (2 * pr, 2 * pr + 1):
                dec = jnp.exp2(jnp.where(causal, seg[:, h * T:(h + 1) * T], -jnp.inf))
                mix.append((cb * dec).astype(BF16))
            x_pair = x_dt_bf[:, pr * V7X_LANES:(pr + 1) * V7X_LANES]
            zero = jnp.zeros_like(x_pair)
            x2 = jnp.concatenate([jnp.where(half, x_pair, zero), jnp.where(half, zero, x_pair)], axis=0)
            y_pairs.append(jnp.dot(jnp.concatenate(mix, axis=1), x2, preferred_element_type=F32))
        y_diag = jnp.concatenate(y_pairs, axis=1)

        y_off = jnp.dot(cm_c, st.astype(BF16), preferred_element_type=F32) * ecum_exp
        xw = (xs_c * w_exp).astype(BF16)
        states_t = lax.dot_general(bm_c, xw, tn, preferred_element_type=F32)
        st = st * ecum_exp[T - 1:T, :] + states_t
        y_rows.append(y_diag + y_off)

    st_ref[...] = st
    y = jnp.concatenate(y_rows, axis=0) + xs * dsk_ref[...]
    y = y * _silu_tanh(z_ref[...])
    y = y * lax.rsqrt(jnp.mean(y * y, axis=-1, keepdims=True) + EPS) * nw_ref[...]
    o_ref[...] = y.astype(o_ref.dtype)


def _ssm(proj, dt_t, conv_w, conv_b, dt_bias, a_log, d_skip, ssm_norm, batch, seq):
    R = SSD_ROWS
    nc = seq // R
    gw = GROUP_WIDTH
    hg = HEADS_PER_GROUP
    dsk = jnp.repeat(d_skip, SSM_HEAD_DIM)[None, :]
    esel, dsel = _ssd_constants()
    row = lambda b, g, c: b * nc + c
    in_specs = [
        pl.BlockSpec((R, gw), lambda b, g, c: (row(b, g, c), Z_OFF // gw + g)),
        pl.BlockSpec((R, gw), lambda b, g, c: (row(b, g, c), XS_OFF // gw + g)),
        pl.BlockSpec((R, SSM_STATE), lambda b, g, c: (row(b, g, c), BM_OFF // SSM_STATE + g)),
        pl.BlockSpec((R, SSM_STATE), lambda b, g, c: (row(b, g, c), CM_OFF // SSM_STATE + g)),
        pl.BlockSpec((hg, R), lambda b, g, c: (g, row(b, g, c))),
        pl.BlockSpec((CONV_WIDTH, gw), lambda b, g, c: (0, g)),
        pl.BlockSpec((CONV_WIDTH, SSM_STATE), lambda b, g, c: (0, SSM_WIDTH // SSM_STATE + g)),
        pl.BlockSpec((CONV_WIDTH, SSM_STATE), lambda b, g, c: (0, SSM_WIDTH // SSM_STATE + SSM_GROUPS + g)),
        pl.BlockSpec((1, gw), lambda b, g, c: (0, g)),
        pl.BlockSpec((1, SSM_STATE), lambda b, g, c: (0, SSM_WIDTH // SSM_STATE + g)),
        pl.BlockSpec((1, SSM_STATE), lambda b, g, c: (0, SSM_WIDTH // SSM_STATE + SSM_GROUPS + g)),
        pl.BlockSpec((hg, 1), lambda b, g, c: (g, 0)),
        pl.BlockSpec((hg, 1), lambda b, g, c: (g, 0)),
        pl.BlockSpec((1, gw), lambda b, g, c: (0, g)),
        pl.BlockSpec((1, gw), lambda b, g, c: (0, g)),
        pl.BlockSpec(esel.shape, lambda b, g, c: (0, 0)),
        pl.BlockSpec(dsel.shape, lambda b, g, c: (0, 0)),
    ]
    return pl.pallas_call(
        _ssm_kernel,
        grid=(batch, SSM_GROUPS, nc),
        in_specs=in_specs,
        out_specs=pl.BlockSpec((R, gw), lambda b, g, c: (row(b, g, c), g)),
        out_shape=jax.ShapeDtypeStruct((batch * seq, SSM_WIDTH), BF16),
        scratch_shapes=[pltpu.VMEM((V7X_SUBLANES + R, gw), F32),
                        pltpu.VMEM((V7X_SUBLANES + R, SSM_STATE), F32),
                        pltpu.VMEM((V7X_SUBLANES + R, SSM_STATE), F32),
                        pltpu.VMEM((SSM_STATE, gw), F32)],
        compiler_params=_params(3),
        name="ssd_mixer",
    )(proj, proj, proj, proj, dt_t,
      conv_w, conv_w, conv_w, conv_b, conv_b, conv_b,
      dt_bias.reshape(SSM_HEADS, 1), a_log.reshape(SSM_HEADS, 1), dsk, ssm_norm, esel, dsel)


def _outproj_kernel(att_ref, ssm_ref, wa_ref, ws_ref, x_ref, post_ref, x1_ref, *, tn, n_steps):
    n = pl.program_id(1)
    mixed = (jnp.dot(att_ref[...], wa_ref[...], preferred_element_type=F32)
             + jnp.dot(ssm_ref[...], ws_ref[...], preferred_element_type=F32))
    x1_ref[:, pl.ds(pl.multiple_of(n * tn, tn), tn)] = mixed

    @pl.when(n == n_steps - 1)
    def _():
        y = x1_ref[...]
        x1_ref[...] = x_ref[...] + y * lax.rsqrt(jnp.mean(y * y, axis=-1, keepdims=True) + EPS) * post_ref[...]


def _out_proj(att, ssm, w_out_bf16, x2d, post_w):
    n_tok = x2d.shape[0]
    tm, tn = 512, 512
    n_steps = D_MODEL // tn
    kern = functools.partial(_outproj_kernel, tn=tn, n_steps=n_steps)
    return pl.pallas_call(
        kern,
        grid=(n_tok // tm, n_steps),
        in_specs=[pl.BlockSpec((tm, ATT_WIDTH), lambda m, n: (m, 0)),
                  pl.BlockSpec((tm, SSM_WIDTH), lambda m, n: (m, 0)),
                  pl.BlockSpec((ATT_WIDTH, tn), lambda m, n: (0, n)),
                  pl.BlockSpec((SSM_WIDTH, tn), lambda m, n: (1, n)),
                  pl.BlockSpec((tm, D_MODEL), lambda m, n: (m, 0)),
                  pl.BlockSpec((1, D_MODEL), lambda m, n: (0, 0))],
        out_specs=pl.BlockSpec((tm, D_MODEL), lambda m, n: (m, 0)),
        out_shape=jax.ShapeDtypeStruct((n_tok, D_MODEL), F32),
        compiler_params=_params(2),
        name="out_proj",
    )(att, ssm, w_out_bf16, w_out_bf16, x2d, post_w)


FFN_TM = 512
FFN_TF = 256
FFN_FIRST_TF = 128


def _ffn_tile_step(f, n_steps, x1_ref, pre_ref, load_weights, post_ref, o_ref, h_ref):
    @pl.when(f == 0)
    def _():
        o_ref[...] = jnp.zeros_like(o_ref)
        x1 = x1_ref[...]
        h = x1 * lax.rsqrt(jnp.mean(x1 * x1, axis=-1, keepdims=True) + EPS) * pre_ref[...]
        h_ref[...] = h.astype(h_ref.dtype)

    wg, wu, wd = load_weights()
    h = h_ref[...]
    g = jnp.dot(h, wg, preferred_element_type=F32)
    u = jnp.dot(h, wu, preferred_element_type=F32)
    act = (_silu(g) * u).astype(BF16)
    o_ref[...] += jnp.dot(act, wd, preferred_element_type=F32)

    @pl.when(f == n_steps - 1)
    def _():
        y = o_ref[...]
        o_ref[...] = x1_ref[...] + y * lax.rsqrt(jnp.mean(y * y, axis=-1, keepdims=True) + EPS) * post_ref[...]


def _ffn_kernel(x1_ref, pre_ref, wg_ref, wu_ref, wd_ref, post_ref, o_ref, h_ref, *, n_steps):
    _ffn_tile_step(pl.program_id(1), n_steps, x1_ref, pre_ref,
                   lambda: (wg_ref[...], wu_ref[...], wd_ref[...]), post_ref, o_ref, h_ref)


def _ffn(x1, pre_w, wg, wu, wd, post_w):
    n_tok = x1.shape[0]
    tm, tf = FFN_TM, FFN_TF
    n_steps = D_FF // tf
    return pl.pallas_call(
        functools.partial(_ffn_kernel, n_steps=n_steps),
        grid=(n_tok // tm, n_steps),
        in_specs=[pl.BlockSpec((tm, D_MODEL), lambda m, f: (m, 0)),
                  pl.BlockSpec((1, D_MODEL), lambda m, f: (0, 0)),
                  pl.BlockSpec((D_MODEL, tf), lambda m, f: (0, f)),
                  pl.BlockSpec((D_MODEL, tf), lambda m, f: (0, f)),
                  pl.BlockSpec((tf, D_MODEL), lambda m, f: (f, 0)),
                  pl.BlockSpec((1, D_MODEL), lambda m, f: (0, 0))],
        out_specs=pl.BlockSpec((tm, D_MODEL), lambda m, f: (m, 0)),
        out_shape=jax.ShapeDtypeStruct((n_tok, D_MODEL), F32),
        scratch_shapes=[pltpu.VMEM((tm, D_MODEL), BF16)],
        compiler_params=_params(2),
        name="swiglu_ffn",
    )(x1, pre_w, wg, wu, wd, post_w)


def kernel(x, positions, mix_pre_norm, w_in, conv_w, conv_b, dt_bias, a_log, d_skip, ssm_norm,
           w_out, mix_post_norm, ffn_pre_norm, w_gate, w_up, w_down, ffn_post_norm):
    batch, seq, _ = x.shape
    depth = w_in.shape[0]
    x2d = x.reshape(batch * seq, D_MODEL)
    cos, sin = _rope_tables(positions)
    for layer in range(depth):
        w_in_t = jnp.swapaxes(w_in[layer], 0, 1).astype(BF16)
        proj, dt_t = _in_proj(x2d, mix_pre_norm[layer][None, :], w_in_t)
        att = _attention(proj, cos, sin, batch, seq)
        ssm = _ssm(proj, dt_t, conv_w[layer], conv_b[layer][None, :], dt_bias[layer], a_log[layer],
                   d_skip[layer], ssm_norm[layer][None, :], batch, seq)
        x1 = _out_proj(att, ssm, w_out[layer].astype(BF16), x2d, mix_post_norm[layer][None, :])
        x2d = _ffn(x1, ffn_pre_norm[layer][None, :], w_gate[layer].astype(BF16), w_up[layer].astype(BF16),
                   w_down[layer].astype(BF16), ffn_post_norm[layer][None, :])
    return x2d.reshape(batch, seq, D_MODEL)
```

```python
import functools
import math

import jax
import jax.numpy as jnp
from jax import lax
from jax.experimental import pallas as pl
from jax.experimental.pallas import tpu as pltpu

F32 = jnp.float32
BF16 = jnp.bfloat16

D_MODEL = 4096
N_HEADS = 16
N_KV_HEADS = 4
KV_GROUP = N_HEADS // N_KV_HEADS
HEAD_DIM = 128
MOBA_BLOCK = 256
MOBA_TOPK = 3
ROPE_THETA = 10000.0
ATT_WIDTH = N_HEADS * HEAD_DIM
KV_DIM = N_KV_HEADS * HEAD_DIM
SSM_WIDTH = 2048
SSM_HEAD_DIM = 64
SSM_HEADS = SSM_WIDTH // SSM_HEAD_DIM
SSM_GROUPS = 4
HEADS_PER_GROUP = SSM_HEADS // SSM_GROUPS
GROUP_WIDTH = SSM_WIDTH // SSM_GROUPS
SSM_STATE = 128
CONV_WIDTH = 4
D_FF = 11008
EPS = 1e-6

Q_OFF = 0
K_OFF = ATT_WIDTH
V_OFF = K_OFF + KV_DIM
Z_OFF = V_OFF + KV_DIM
XS_OFF = Z_OFF + SSM_WIDTH
BM_OFF = XS_OFF + SSM_WIDTH
CM_OFF = BM_OFF + SSM_GROUPS * SSM_STATE
DT_OFF = CM_OFF + SSM_GROUPS * SSM_STATE
PROJ_COLS = DT_OFF

V7X_LANES = 128
V7X_SUBLANES = 8
V7X_VMEM_LIMIT_BYTES = 60000 * 1024

MASK_BIAS = -1e30


def _params(n_axes, vmem_bytes=V7X_VMEM_LIMIT_BYTES):
    return pltpu.CompilerParams(dimension_semantics=("arbitrary",) * n_axes,
                                vmem_limit_bytes=vmem_bytes)


def _sigmoid(x):
    return 1.0 / (1.0 + jnp.exp(-x))


def _silu(x):
    return x * _sigmoid(x)


def _silu_tanh(x):
    h = 0.5 * x
    return h + h * jnp.tanh(h)


def _split3_f32(x):
    hi = x.astype(BF16).astype(F32)
    r1 = x - hi
    mid = r1.astype(BF16).astype(F32)
    lo = (r1 - mid).astype(BF16).astype(F32)
    return hi, mid, lo


def _rope_kernel(pos_ref, invf_ref, cos_ref, sin_ref):
    ang = pos_ref[...].astype(F32) * invf_ref[...]
    lane = lax.broadcasted_iota(jnp.int32, ang.shape, 1)
    cos_ref[...] = jnp.cos(ang)
    sin_ref[...] = jnp.where(lane < HEAD_DIM // 2, -jnp.sin(ang), jnp.sin(ang))


def _rope_tables(positions):
    n = positions.size
    half = HEAD_DIM // 2
    inv_freq = ROPE_THETA ** (-jnp.arange(half, dtype=F32) / half)
    invf = jnp.concatenate([inv_freq, inv_freq])[None, :]
    pos = positions.reshape(n, 1)
    tm = 1024
    return pl.pallas_call(
        _rope_kernel,
        grid=(n // tm,),
        in_specs=[pl.BlockSpec((tm, 1), lambda i: (i, 0)),
                  pl.BlockSpec((1, HEAD_DIM), lambda i: (0, 0))],
        out_specs=[pl.BlockSpec((tm, HEAD_DIM), lambda i: (i, 0)),
                   pl.BlockSpec((tm, HEAD_DIM), lambda i: (i, 0))],
        out_shape=[jax.ShapeDtypeStruct((n, HEAD_DIM), F32)] * 2,
        compiler_params=_params(1),
        name="rope_tables",
    )(pos, invf)


def _rope(t, cos, sin_signed):
    return t * cos + pltpu.roll(t, HEAD_DIM // 2, 1) * sin_signed


def _inproj_kernel(x_ref, nw_ref, w_ref, wdt_ref, proj_ref, dtT_ref, h_ref):
    @pl.when(pl.program_id(1) == 0)
    def _():
        x = x_ref[...]
        h = x * lax.rsqrt(jnp.mean(x * x, axis=-1, keepdims=True) + EPS) * nw_ref[...]
        hb = h.astype(BF16)
        h_ref[...] = hb
        dtT_ref[...] = lax.dot_general(wdt_ref[...], hb, (((1,), (1,)), ((), ())),
                                       preferred_element_type=F32)

    proj_ref[...] = lax.dot_general(h_ref[...], w_ref[...], (((1,), (1,)), ((), ())),
                                    preferred_element_type=F32)


def _in_proj(x2d, norm_w, w_in_t):
    n_tok = x2d.shape[0]
    tm, tn = 512, 1024
    return pl.pallas_call(
        _inproj_kernel,
        grid=(n_tok // tm, PROJ_COLS // tn),
        in_specs=[pl.BlockSpec((tm, D_MODEL), lambda m, n: (m, 0)),
                  pl.BlockSpec((1, D_MODEL), lambda m, n: (0, 0)),
                  pl.BlockSpec((tn, D_MODEL), lambda m, n: (n, 0)),
                  pl.BlockSpec((SSM_HEADS, D_MODEL), lambda m, n: (DT_OFF // SSM_HEADS, 0))],
        out_specs=[pl.BlockSpec((tm, tn), lambda m, n: (m, n)),
                   pl.BlockSpec((SSM_HEADS, tm), lambda m, n: (0, m))],
        out_shape=[jax.ShapeDtypeStruct((n_tok, PROJ_COLS), F32),
                   jax.ShapeDtypeStruct((SSM_HEADS, n_tok), F32)],
        scratch_shapes=[pltpu.VMEM((tm, D_MODEL), BF16)],
        compiler_params=_params(2),
        name="in_proj",
    )(x2d, norm_w, w_in_t, w_in_t)


def _attn_block_step(n_visible, q_aug, kr_ref, vt_ref, kg_ref, o_ref):
    nt = (((1,), (1,)), ((), ()))
    gate_rows = 2 * V7X_SUBLANES
    n_past = n_visible - 1
    n_keys = n_visible * MOBA_BLOCK

    lhs = jnp.concatenate([kr_ref[0:n_keys, :], kg_ref[...]], axis=0)
    s_all = lax.dot_general(lhs, q_aug, nt, preferred_element_type=F32)
    gate_t = s_all[n_keys:n_keys + V7X_SUBLANES] + s_all[n_keys + gate_rows:n_keys + gate_rows + V7X_SUBLANES]

    blk = lax.broadcasted_iota(jnp.int32, gate_t.shape, 0)
    past = blk < n_past
    gate_t = jnp.where(past, gate_t, -jnp.inf)
    rank = jnp.zeros(gate_t.shape, F32)
    for jp in range(n_past):
        other = gate_t[jp:jp + 1, :]
        rank = rank + jnp.where(other > gate_t, 1.0, jnp.where((other == gate_t) & (jp < blk), 1.0, 0.0))
    bias_t = jnp.where(past & (rank < MOBA_TOPK), 0.0, MASK_BIAS)

    own = s_all[n_past * MOBA_BLOCK:n_keys]
    kpos = lax.broadcasted_iota(jnp.int32, own.shape, 0)
    qpos = lax.broadcasted_iota(jnp.int32, own.shape, 1) & (MOBA_BLOCK - 1)
    blocks = [s_all[j * MOBA_BLOCK:(j + 1) * MOBA_BLOCK] for j in range(n_past)]
    blocks.append(jnp.where(kpos <= qpos, own, -jnp.inf))
    biases = [bias_t[j:j + 1, :] for j in range(n_past)] + [jnp.zeros((1, own.shape[1]), F32)]

    m = jnp.max(blocks[0], axis=0, keepdims=True) + biases[0]
    for blk_s, b in zip(blocks[1:], biases[1:]):
        m = jnp.maximum(m, jnp.max(blk_s, axis=0, keepdims=True) + b)
    probs = [jnp.exp2(blk_s - (m - b)) for blk_s, b in zip(blocks, biases)]
    l = jnp.sum(probs[0], axis=0, keepdims=True)
    for pj in probs[1:]:
        l = l + jnp.sum(pj, axis=0, keepdims=True)
    p_all = jnp.concatenate([pj.astype(BF16) for pj in probs], axis=0)
    acc = jnp.dot(vt_ref[:, 0:n_keys], p_all, preferred_element_type=F32)
    out = (acc * (1.0 / l)).T
    for g in range(KV_GROUP):
        o_ref[:, g * HEAD_DIM:(g + 1) * HEAD_DIM] = out[g * MOBA_BLOCK:(g + 1) * MOBA_BLOCK].astype(o_ref.dtype)


def _attn_kernel(q_ref, k_ref, v_ref, cq_ref, sq_ref, ck_ref, sk_ref, o_ref,
                 kr_ref, vt_ref, kg_ref, *, n_blocks):
    i = pl.program_id(2)
    scale = HEAD_DIM ** -0.5 * math.log2(math.e)
    gate_rows = 2 * V7X_SUBLANES

    @pl.when(i == 0)
    def _():
        kr = _rope(k_ref[...], ck_ref[...], sk_ref[...])
        kr_ref[:, :HEAD_DIM] = kr.astype(BF16)
        kr_ref[:, HEAD_DIM:] = jnp.zeros(kr.shape, BF16)
        vt_ref[...] = v_ref[...].T.astype(BF16)
        km = jnp.mean(kr.reshape(n_blocks, MOBA_BLOCK, HEAD_DIM), axis=1)
        km_hi, km_mid, _ = _split3_f32(km)
        zpad = jnp.zeros((gate_rows - n_blocks, 2 * HEAD_DIM), F32)
        kg_ref[...] = jnp.concatenate(
            [jnp.concatenate([km_hi, km_hi], axis=1), zpad,
             jnp.concatenate([km_mid, jnp.zeros_like(km_mid)], axis=1), zpad], axis=0).astype(BF16)

    cq = cq_ref[...]
    sq = sq_ref[...]
    q_all = jnp.concatenate(
        [_rope(q_ref[:, g * HEAD_DIM:(g + 1) * HEAD_DIM], cq, sq) * scale for g in range(KV_GROUP)],
        axis=0)
    q_hi = q_all.astype(BF16)
    q_mid = (q_all - q_hi.astype(F32)).astype(BF16)
    q_aug = jnp.concatenate([q_hi, q_mid], axis=1)

    for n_visible in range(1, n_blocks + 1):
        pl.when(i == n_visible - 1)(
            functools.partial(_attn_block_step, n_visible, q_aug, kr_ref, vt_ref, kg_ref, o_ref))


def _attention(proj, cos, sin, batch, seq):
    n_blocks = seq // MOBA_BLOCK
    assert n_blocks <= V7X_SUBLANES, "block gate layout holds at most 8 key blocks"
    qw = KV_GROUP * HEAD_DIM
    kern = functools.partial(_attn_kernel, n_blocks=n_blocks)
    return pl.pallas_call(
        kern,
        grid=(batch, N_KV_HEADS, n_blocks),
        in_specs=[
            pl.BlockSpec((MOBA_BLOCK, qw), lambda b, h, i: (b * n_blocks + i, h)),
            pl.BlockSpec((seq, HEAD_DIM), lambda b, h, i: (b, K_OFF // HEAD_DIM + h)),
            pl.BlockSpec((seq, HEAD_DIM), lambda b, h, i: (b, V_OFF // HEAD_DIM + h)),
            pl.BlockSpec((MOBA_BLOCK, HEAD_DIM), lambda b, h, i: (b * n_blocks + i, 0)),
            pl.BlockSpec((MOBA_BLOCK, HEAD_DIM), lambda b, h, i: (b * n_blocks + i, 0)),
            pl.BlockSpec((seq, HEAD_DIM), lambda b, h, i: (b, 0)),
            pl.BlockSpec((seq, HEAD_DIM), lambda b, h, i: (b, 0)),
        ],
        out_specs=pl.BlockSpec((MOBA_BLOCK, qw), lambda b, h, i: (b * n_blocks + i, h)),
        out_shape=jax.ShapeDtypeStruct((batch * seq, ATT_WIDTH), BF16),
        scratch_shapes=[pltpu.VMEM((seq, 2 * HEAD_DIM), BF16),
                        pltpu.VMEM((HEAD_DIM, seq), BF16),
                        pltpu.VMEM((4 * V7X_SUBLANES, 2 * HEAD_DIM), BF16)],
        compiler_params=_params(3),
        name="moba_attention",
    )(proj, proj, proj, cos, sin, cos, sin)


SSD_ROWS = 512
SSD_SUB = 128
N_EXPAND = 3


def _ssd_constants():
    hg = HEADS_PER_GROUP
    n_rows = N_EXPAND * 3 * hg
    r = jnp.arange(n_rows + (-n_rows) % (2 * V7X_SUBLANES))[:, None]
    c = jnp.arange(N_EXPAND * GROUP_WIDTH)[None, :]
    e = (r < n_rows) & (r // (3 * hg) == c // GROUP_WIDTH) & (r % hg == (c % GROUP_WIDTH) // SSM_HEAD_DIM)
    r2 = jnp.arange(6 * hg)[:, None]
    c2 = jnp.arange(hg * SSD_SUB)[None, :]
    d = (r2 % hg) == (c2 // SSD_SUB)
    return e.astype(BF16), d.astype(F32)


def _ssm_kernel(z_ref, xs_ref, bm_ref, cm_ref, dtT_ref,
                cwx_ref, cwb_ref, cwc_ref, cbx_ref, cbb_ref, cbc_ref,
                dtb_ref, alog_ref, dsk_ref, nw_ref, esel_ref, dsel_ref, o_ref,
                xt_ref, bt_ref, ct_ref, st_ref):
    c = pl.program_id(2)
    R = SSD_ROWS
    T = SSD_SUB
    hg = HEADS_PER_GROUP
    tail = V7X_SUBLANES
    tn = (((0,), (0,)), ((), ()))
    nt = (((1,), (1,)), ((), ()))

    @pl.when(c == 0)
    def _():
        for ext_ref in (xt_ref, bt_ref, ct_ref):
            ext_ref[0:tail, :] = jnp.zeros((tail, ext_ref.shape[1]), F32)
        st_ref[...] = jnp.zeros_like(st_ref)

    def conv_silu(u_ref, ext_ref, w_ref, b_ref):
        ext_ref[tail:tail + R, :] = u_ref[...]
        acc = b_ref[...]
        for k in range(CONV_WIDTH):
            lo = tail - (CONV_WIDTH - 1) + k
            acc = acc + w_ref[k:k + 1, :] * ext_ref[lo:lo + R, :]
        ext_ref[0:tail, :] = ext_ref[R:R + tail, :]
        return _silu_tanh(acc)

    xs = conv_silu(xs_ref, xt_ref, cwx_ref, cbx_ref)
    bm_bf = conv_silu(bm_ref, bt_ref, cwb_ref, cbb_ref).astype(BF16)
    cm_bf = conv_silu(cm_ref, ct_ref, cwc_ref, cbc_ref).astype(BF16)

    dt_in = dtT_ref[...] + dtb_ref[...]
    dt_t = jnp.maximum(dt_in, 0.0) + jnp.log1p(jnp.exp(-jnp.abs(dt_in)))
    adt_t = dt_t * (-jnp.exp(alog_ref[...]))

    lane = lax.broadcasted_iota(jnp.int32, (hg, T), 1)
    li = lax.broadcasted_iota(jnp.int32, (T, T), 0)
    si = lax.broadcasted_iota(jnp.int32, (T, T), 1)
    causal = si <= li
    half = lax.broadcasted_iota(jnp.int32, (T, V7X_LANES), 1) < SSM_HEAD_DIM
    ones = jnp.ones((hg, T), F32)
    esel = esel_ref[...]
    dsel = dsel_ref[...]

    st = st_ref[...]
    y_rows = []
    for sc in range(R // T):
        r0 = sc * T
        acum = adt_t[:, r0:r0 + T]
        shift = 1
        while shift < T:
            acum = acum + jnp.where(lane >= shift, pltpu.roll(acum, shift, 1), 0.0)
            shift *= 2
        dt_c = dt_t[:, r0:r0 + T]
        a_last = acum[:, T - 1:T]
        ecum = jnp.exp(acum)
        w = dt_c * jnp.exp(a_last - acum)

        parts = []
        for v in (dt_c, w, ecum):
            parts.extend(_split3_f32(v))
        pad_rows = esel.shape[0] - len(parts) * hg
        if pad_rows:
            parts.append(jnp.zeros((pad_rows, T), F32))
        stacked = jnp.concatenate(parts, axis=0).astype(BF16)
        expd = lax.dot_general(stacked, esel, tn, preferred_element_type=F32)
        dt_exp = expd[:, :GROUP_WIDTH]
        w_exp = expd[:, GROUP_WIDTH:2 * GROUP_WIDTH]
        ecum_exp = expd[:, 2 * GROUP_WIDTH:]

        a_hi, a_mid, a_lo = _split3_f32(acum * math.log2(math.e))
        lhs_t = jnp.concatenate([a_hi, a_mid, a_lo, ones, ones, ones], axis=0).astype(BF16)
        rhs = jnp.concatenate([ones, ones, ones, -a_hi, -a_mid, -a_lo], axis=0)
        rhs = (jnp.concatenate([rhs] * hg, axis=1) * dsel).astype(BF16)
        seg = lax.dot_general(lhs_t, rhs, tn, preferred_element_type=F32)

        xs_c = xs[r0:r0 + T]
        x_dt_bf = (xs_c * dt_exp).astype(BF16)
        bm_c = bm_bf[r0:r0 + T]
        cm_c = cm_bf[r0:r0 + T]
        cb = lax.dot_general(cm_c, bm_c, nt, preferred_element_type=F32)

        y_pairs = []
        for pr in range(hg // 2):
            mix = []
            for h in (2 * pr, 2 * pr + 1):
                dec = jnp.exp2(jnp.where(causal, seg[:, h * T:(h + 1) * T], -jnp.inf))
                mix.append((cb * dec).astype(BF16))
            x_pair = x_dt_bf[:, pr * V7X_LANES:(pr + 1) * V7X_LANES]
            zero = jnp.zeros_like(x_pair)
            x2 = jnp.concatenate([jnp.where(half, x_pair, zero), jnp.where(half, zero, x_pair)], axis=0)
            y_pairs.append(jnp.dot(jnp.concatenate(mix, axis=1), x2, preferred_element_type=F32))
        y_diag = jnp.concatenate(y_pairs, axis=1)

        y_off = jnp.dot(cm_c, st.astype(BF16), preferred_element_type=F32) * ecum_exp
        xw = (xs_c * w_exp).astype(BF16)
        states_t = lax.dot_general(bm_c, xw, tn, preferred_element_type=F32)
        st = st * ecum_exp[T - 1:T, :] + states_t
        y_rows.append(y_diag + y_off)

    st_ref[...] = st
    y = jnp.concatenate(y_rows, axis=0) + xs * dsk_ref[...]
    y = y * _silu_tanh(z_ref[...])
    y = y * lax.rsqrt(jnp.mean(y * y, axis=-1, keepdims=True) + EPS) * nw_ref[...]
    o_ref[...] = y.astype(o_ref.dtype)


def _ssm(proj, dt_t, conv_w, conv_b, dt_bias, a_log, d_skip, ssm_norm, batch, seq):
    R = SSD_ROWS
    nc = seq // R
    gw = GROUP_WIDTH
    hg = HEADS_PER_GROUP
    dsk = jnp.repeat(d_skip, SSM_HEAD_DIM)[None, :]
    esel, dsel = _ssd_constants()
    row = lambda b, g, c: b * nc + c
    in_specs = [
        pl.BlockSpec((R, gw), lambda b, g, c: (row(b, g, c), Z_OFF // gw + g)),
        pl.BlockSpec((R, gw), lambda b, g, c: (row(b, g, c), XS_OFF // gw + g)),
        pl.BlockSpec((R, SSM_STATE), lambda b, g, c: (row(b, g, c), BM_OFF // SSM_STATE + g)),
        pl.BlockSpec((R, SSM_STATE), lambda b, g, c: (row(b, g, c), CM_OFF // SSM_STATE + g)),
        pl.BlockSpec((hg, R), lambda b, g, c: (g, row(b, g, c))),
        pl.BlockSpec((CONV_WIDTH, gw), lambda b, g, c: (0, g)),
        pl.BlockSpec((CONV_WIDTH, SSM_STATE), lambda b, g, c: (0, SSM_WIDTH // SSM_STATE + g)),
        pl.BlockSpec((CONV_WIDTH, SSM_STATE), lambda b, g, c: (0, SSM_WIDTH // SSM_STATE + SSM_GROUPS + g)),
        pl.BlockSpec((1, gw), lambda b, g, c: (0, g)),
        pl.BlockSpec((1, SSM_STATE), lambda b, g, c: (0, SSM_WIDTH // SSM_STATE + g)),
        pl.BlockSpec((1, SSM_STATE), lambda b, g, c: (0, SSM_WIDTH // SSM_STATE + SSM_GROUPS + g)),
        pl.BlockSpec((hg, 1), lambda b, g, c: (g, 0)),
        pl.BlockSpec((hg, 1), lambda b, g, c: (g, 0)),
        pl.BlockSpec((1, gw), lambda b, g, c: (0, g)),
        pl.BlockSpec((1, gw), lambda b, g, c: (0, g)),
        pl.BlockSpec(esel.shape, lambda b, g, c: (0, 0)),
        pl.BlockSpec(dsel.shape, lambda b, g, c: (0, 0)),
    ]
    return pl.pallas_call(
        _ssm_kernel,
        grid=(batch, SSM_GROUPS, nc),
        in_specs=in_specs,
        out_specs=pl.BlockSpec((R, gw), lambda b, g, c: (row(b, g, c), g)),
        out_shape=jax.ShapeDtypeStruct((batch * seq, SSM_WIDTH), BF16),
        scratch_shapes=[pltpu.VMEM((V7X_SUBLANES + R, gw), F32),
                        pltpu.VMEM((V7X_SUBLANES + R, SSM_STATE), F32),
                        pltpu.VMEM((V7X_SUBLANES + R, SSM_STATE), F32),
                        pltpu.VMEM((SSM_STATE, gw), F32)],
        compiler_params=_params(3),
        name="ssd_mixer",
    )(proj, proj, proj, proj, dt_t,
      conv_w, conv_w, conv_w, conv_b, conv_b, conv_b,
      dt_bias.reshape(SSM_HEADS, 1), a_log.reshape(SSM_HEADS, 1), dsk, ssm_norm, esel, dsel)


def _outproj_kernel(att_ref, ssm_ref, wa_ref, ws_ref, x_ref, post_ref, x1_ref, *, tn, n_steps):
    n = pl.program_id(1)
    mixed = (jnp.dot(att_ref[...], wa_ref[...], preferred_element_type=F32)
             + jnp.dot(ssm_ref[...], ws_ref[...], preferred_element_type=F32))
    x1_ref[:, pl.ds(pl.multiple_of(n * tn, tn), tn)] = mixed

    @pl.when(n == n_steps - 1)
    def _():
        y = x1_ref[...]
        x1_ref[...] = x_ref[...] + y * lax.rsqrt(jnp.mean(y * y, axis=-1, keepdims=True) + EPS) * post_ref[...]


def _out_proj(att, ssm, w_out_bf16, x2d, post_w):
    n_tok = x2d.shape[0]
    tm, tn = 512, 512
    n_steps = D_MODEL // tn
    kern = functools.partial(_outproj_kernel, tn=tn, n_steps=n_steps)
    return pl.pallas_call(
        kern,
        grid=(n_tok // tm, n_steps),
        in_specs=[pl.BlockSpec((tm, ATT_WIDTH), lambda m, n: (m, 0)),
                  pl.BlockSpec((tm, SSM_WIDTH), lambda m, n: (m, 0)),
                  pl.BlockSpec((ATT_WIDTH, tn), lambda m, n: (0, n)),
                  pl.BlockSpec((SSM_WIDTH, tn), lambda m, n: (1, n)),
                  pl.BlockSpec((tm, D_MODEL), lambda m, n: (m, 0)),
                  pl.BlockSpec((1, D_MODEL), lambda m, n: (0, 0))],
        out_specs=pl.BlockSpec((tm, D_MODEL), lambda m, n: (m, 0)),
        out_shape=jax.ShapeDtypeStruct((n_tok, D_MODEL), F32),
        compiler_params=_params(2),
        name="out_proj",
    )(att, ssm, w_out_bf16, w_out_bf16, x2d, post_w)


FFN_TM = 512
FFN_TF = 256
FFN_FIRST_TF = 128


def _ffn_tile_step(f, n_steps, x1_ref, pre_ref, load_weights, post_ref, o_ref, h_ref):
    @pl.when(f == 0)
    def _():
        o_ref[...] = jnp.zeros_like(o_ref)
        x1 = x1_ref[...]
        h = x1 * lax.rsqrt(jnp.mean(x1 * x1, axis=-1, keepdims=True) + EPS) * pre_ref[...]
        h_ref[...] = h.astype(h_ref.dtype)

    wg, wu, wd = load_weights()
    h = h_ref[...]
    g = jnp.dot(h, wg, preferred_element_type=F32)
    u = jnp.dot(h, wu, preferred_element_type=F32)
    act = (_silu(g) * u).astype(BF16)
    o_ref[...] += jnp.dot(act, wd, preferred_element_type=F32)

    @pl.when(f == n_steps - 1)
    def _():
        y = o_ref[...]
        o_ref[...] = x1_ref[...] + y * lax.rsqrt(jnp.mean(y * y, axis=-1, keepdims=True) + EPS) * post_ref[...]


def _ffn_first_kernel(x1_ref, pre_ref, wg_ref, wu_ref, wd_ref, post_ref,
                      o_ref, wgb_ref, wub_ref, wdb_ref, h_ref, *, n_steps):
    def load_weights():
        wg = wg_ref[...].astype(BF16)
        wu = wu_ref[...].astype(BF16)
        wd = wd_ref[...].astype(BF16)
        wgb_ref[...] = wg
        wub_ref[...] = wu
        wdb_ref[...] = wd
        return wg, wu, wd

    _ffn_tile_step(pl.program_id(0), n_steps, x1_ref, pre_ref, load_weights, post_ref, o_ref, h_ref)


def _ffn_rest_kernel(x1_ref, pre_ref, wg_ref, wu_ref, wd_ref, post_ref, first_hbm, o_ref, h_ref, sem,
                     *, n_steps):
    m = pl.program_id(0)
    f = pl.program_id(1)

    @pl.when((m == 0) & (f == 0))
    def _():
        cp = pltpu.make_async_copy(first_hbm, o_ref, sem)
        cp.start()
        cp.wait()

    @pl.when(m > 0)
    def _():
        _ffn_tile_step(f, n_steps, x1_ref, pre_ref,
                       lambda: (wg_ref[...], wu_ref[...], wd_ref[...]), post_ref, o_ref, h_ref)


def _ffn(x1, pre_w, wg, wu, wd, post_w):
    n_tok = x1.shape[0]
    tm, tf, tf1 = FFN_TM, FFN_TF, FFN_FIRST_TF
    n_steps = D_FF // tf
    single = pl.Buffered(1)
    first, wg_bf, wu_bf, wd_bf = pl.pallas_call(
        functools.partial(_ffn_first_kernel, n_steps=D_FF // tf1),
        grid=(D_FF // tf1,),
        in_specs=[pl.BlockSpec((tm, D_MODEL), lambda f: (0, 0), pipeline_mode=single),
                  pl.BlockSpec((1, D_MODEL), lambda f: (0, 0)),
                  pl.BlockSpec((D_MODEL, tf1), lambda f: (0, f)),
                  pl.BlockSpec((D_MODEL, tf1), lambda f: (0, f)),
                  pl.BlockSpec((tf1, D_MODEL), lambda f: (f, 0)),
                  pl.BlockSpec((1, D_MODEL), lambda f: (0, 0))],
        out_specs=[pl.BlockSpec((tm, D_MODEL), lambda f: (0, 0), pipeline_mode=single),
                   pl.BlockSpec((D_MODEL, tf1), lambda f: (0, f)),
                   pl.BlockSpec((D_MODEL, tf1), lambda f: (0, f)),
                   pl.BlockSpec((tf1, D_MODEL), lambda f: (f, 0))],
        out_shape=[jax.ShapeDtypeStruct((tm, D_MODEL), F32),
                   jax.ShapeDtypeStruct(wg.shape, BF16),
                   jax.ShapeDtypeStruct(wu.shape, BF16),
                   jax.ShapeDtypeStruct(wd.shape, BF16)],
        scratch_shapes=[pltpu.VMEM((tm, D_MODEL), BF16)],
        compiler_params=_params(1),
        name="swiglu_ffn_first",
    )(x1, pre_w, wg, wu, wd, post_w)

    col = lambda m, f: (0, jnp.where(m == 0, 0, f))
    rowb = lambda m, f: (jnp.where(m == 0, 0, f), 0)
    return pl.pallas_call(
        functools.partial(_ffn_rest_kernel, n_steps=n_steps),
        grid=(n_tok // tm, n_steps),
        in_specs=[pl.BlockSpec((tm, D_MODEL), lambda m, f: (m, 0)),
                  pl.BlockSpec((1, D_MODEL), lambda m, f: (0, 0)),
                  pl.BlockSpec((D_MODEL, tf), col),
                  pl.BlockSpec((D_MODEL, tf), col),
                  pl.BlockSpec((tf, D_MODEL), rowb),
                  pl.BlockSpec((1, D_MODEL), lambda m, f: (0, 0)),
                  pl.BlockSpec(memory_space=pl.ANY)],
        out_specs=pl.BlockSpec((tm, D_MODEL), lambda m, f: (m, 0)),
        out_shape=jax.ShapeDtypeStruct((n_tok, D_MODEL), F32),
        scratch_shapes=[pltpu.VMEM((tm, D_MODEL), BF16), pltpu.SemaphoreType.DMA(())],
        compiler_params=_params(2),
        name="swiglu_ffn",
    )(x1, pre_w, wg_bf, wu_bf, wd_bf, post_w, first)


def kernel(x, positions, mix_pre_norm, w_in, conv_w, conv_b, dt_bias, a_log, d_skip, ssm_norm,
           w_out, mix_post_norm, ffn_pre_norm, w_gate, w_up, w_down, ffn_post_norm):
    batch, seq, _ = x.shape
    depth = w_in.shape[0]
    x2d = x.reshape(batch * seq, D_MODEL)
    cos, sin = _rope_tables(positions)
    for layer in range(depth):
        w_in_t = jnp.swapaxes(w_in[layer], 0, 1).astype(BF16)
        proj, dt_t = _in_proj(x2d, mix_pre_norm[layer][None, :], w_in_t)
        att = _attention(proj, cos, sin, batch, seq)
        ssm = _ssm(proj, dt_t, conv_w[layer], conv_b[layer][None, :], dt_bias[layer], a_log[layer],
                   d_skip[layer], ssm_norm[layer][None, :], batch, seq)
        x1 = _out_proj(att, ssm, w_out[layer].astype(BF16), x2d, mix_post_norm[layer][None, :])
        x2d = _ffn(x1, ffn_pre_norm[layer][None, :], w_gate[layer], w_up[layer], w_down[layer],
                   ffn_post_norm[layer][None, :])
    return x2d.reshape(batch, seq, D_MODEL)
```

```python
import functools
import math

import jax
import jax.numpy as jnp
from jax import lax
from jax.experimental import pallas as pl
from jax.experimental.pallas import tpu as pltpu

F32 = jnp.float32
BF16 = jnp.bfloat16

D_MODEL = 4096
N_HEADS = 16
N_KV_HEADS = 4
KV_GROUP = N_HEADS // N_KV_HEADS
HEAD_DIM = 128
MOBA_BLOCK = 256
MOBA_TOPK = 3
ROPE_THETA = 10000.0
ATT_WIDTH = N_HEADS * HEAD_DIM
KV_DIM = N_KV_HEADS * HEAD_DIM
SSM_WIDTH = 2048
SSM_HEAD_DIM = 64
SSM_HEADS = SSM_WIDTH // SSM_HEAD_DIM
SSM_GROUPS = 4
HEADS_PER_GROUP = SSM_HEADS // SSM_GROUPS
GROUP_WIDTH = SSM_WIDTH // SSM_GROUPS
SSM_STATE = 128
CONV_WIDTH = 4
D_FF = 11008
EPS = 1e-6

Q_OFF = 0
K_OFF = ATT_WIDTH
V_OFF = K_OFF + KV_DIM
Z_OFF = V_OFF + KV_DIM
XS_OFF = Z_OFF + SSM_WIDTH
BM_OFF = XS_OFF + SSM_WIDTH
CM_OFF = BM_OFF + SSM_GROUPS * SSM_STATE
DT_OFF = CM_OFF + SSM_GROUPS * SSM_STATE
PROJ_COLS = DT_OFF

V7X_LANES = 128
V7X_SUBLANES = 8
V7X_MXU_COLS = 256
V7X_VMEM_LIMIT_BYTES = 60000 * 1024

MASK_BIAS = -1e30


def _params(n_axes, vmem_bytes=V7X_VMEM_LIMIT_BYTES):
    return pltpu.CompilerParams(dimension_semantics=("arbitrary",) * n_axes,
                                vmem_limit_bytes=vmem_bytes)


def _sigmoid(x):
    return 1.0 / (1.0 + jnp.exp(-x))


def _silu(x):
    return x * _sigmoid(x)


def _silu_tanh(x):
    h = 0.5 * x
    return h + h * jnp.tanh(h)


def _split3_f32(x):
    hi = x.astype(BF16).astype(F32)
    r1 = x - hi
    mid = r1.astype(BF16).astype(F32)
    lo = (r1 - mid).astype(BF16).astype(F32)
    return hi, mid, lo


def _rope_kernel(pos_ref, invf_ref, cos_ref, sin_ref):
    ang = pos_ref[...].astype(F32) * invf_ref[...]
    lane = lax.broadcasted_iota(jnp.int32, ang.shape, 1)
    cos_ref[...] = jnp.cos(ang)
    sin_ref[...] = jnp.where(lane < HEAD_DIM // 2, -jnp.sin(ang), jnp.sin(ang))


def _rope_tables(positions):
    n = positions.size
    half = HEAD_DIM // 2
    inv_freq = ROPE_THETA ** (-jnp.arange(half, dtype=F32) / half)
    invf = jnp.concatenate([inv_freq, inv_freq])[None, :]
    pos = positions.reshape(n, 1)
    tm = 1024
    return pl.pallas_call(
        _rope_kernel,
        grid=(n // tm,),
        in_specs=[pl.BlockSpec((tm, 1), lambda i: (i, 0)),
                  pl.BlockSpec((1, HEAD_DIM), lambda i: (0, 0))],
        out_specs=[pl.BlockSpec((tm, HEAD_DIM), lambda i: (i, 0)),
                   pl.BlockSpec((tm, HEAD_DIM), lambda i: (i, 0))],
        out_shape=[jax.ShapeDtypeStruct((n, HEAD_DIM), F32)] * 2,
        compiler_params=_params(1),
        name="rope_tables",
    )(pos, invf)


def _rope(t, cos, sin_signed):
    return t * cos + pltpu.roll(t, HEAD_DIM // 2, 1) * sin_signed


def _inproj_kernel(x_ref, nw_ref, w_ref, wdt_ref, proj_ref, dtT_ref, h_ref):
    @pl.when(pl.program_id(1) == 0)
    def _():
        x = x_ref[...]
        h = x * lax.rsqrt(jnp.mean(x * x, axis=-1, keepdims=True) + EPS) * nw_ref[...]
        hb = h.astype(BF16)
        h_ref[...] = hb
        dtT_ref[...] = lax.dot_general(wdt_ref[...], hb, (((1,), (1,)), ((), ())),
                                       preferred_element_type=F32)

    proj_ref[...] = lax.dot_general(h_ref[...], w_ref[...], (((1,), (1,)), ((), ())),
                                    preferred_element_type=F32)


def _in_proj(x2d, norm_w, w_in_t):
    n_tok = x2d.shape[0]
    tm, tn = 512, 1024
    return pl.pallas_call(
        _inproj_kernel,
        grid=(n_tok // tm, PROJ_COLS // tn),
        in_specs=[pl.BlockSpec((tm, D_MODEL), lambda m, n: (m, 0)),
                  pl.BlockSpec((1, D_MODEL), lambda m, n: (0, 0)),
                  pl.BlockSpec((tn, D_MODEL), lambda m, n: (n, 0)),
                  pl.BlockSpec((SSM_HEADS, D_MODEL), lambda m, n: (DT_OFF // SSM_HEADS, 0))],
        out_specs=[pl.BlockSpec((tm, tn), lambda m, n: (m, n)),
                   pl.BlockSpec((SSM_HEADS, tm), lambda m, n: (0, m))],
        out_shape=[jax.ShapeDtypeStruct((n_tok, PROJ_COLS), F32),
                   jax.ShapeDtypeStruct((SSM_HEADS, n_tok), F32)],
        scratch_shapes=[pltpu.VMEM((tm, D_MODEL), BF16)],
        compiler_params=_params(2),
        name="in_proj",
    )(x2d, norm_w, w_in_t, w_in_t)


def _attn_block_step(n_visible, q_aug, kr_ref, vt_ref, kg_ref, o_ref):
    nt = (((1,), (1,)), ((), ()))
    gate_rows = 2 * V7X_SUBLANES
    n_past = n_visible - 1
    n_keys = n_visible * MOBA_BLOCK

    lhs = jnp.concatenate([kr_ref[0:n_keys, :], kg_ref[...]], axis=0)
    s_all = lax.dot_general(lhs, q_aug, nt, preferred_element_type=F32)
    gate_t = s_all[n_keys:n_keys + V7X_SUBLANES] + s_all[n_keys + gate_rows:n_keys + gate_rows + V7X_SUBLANES]

    blk = lax.broadcasted_iota(jnp.int32, gate_t.shape, 0)
    past = blk < n_past
    gate_t = jnp.where(past, gate_t, -jnp.inf)
    rank = jnp.zeros(gate_t.shape, F32)
    for jp in range(n_past):
        other = gate_t[jp:jp + 1, :]
        rank = rank + jnp.where(other > gate_t, 1.0, jnp.where((other == gate_t) & (jp < blk), 1.0, 0.0))
    bias_t = jnp.where(past & (rank < MOBA_TOPK), 0.0, MASK_BIAS)

    own = s_all[n_past * MOBA_BLOCK:n_keys]
    kpos = lax.broadcasted_iota(jnp.int32, own.shape, 0)
    qpos = lax.broadcasted_iota(jnp.int32, own.shape, 1) & (MOBA_BLOCK - 1)
    blocks = [s_all[j * MOBA_BLOCK:(j + 1) * MOBA_BLOCK] for j in range(n_past)]
    blocks.append(jnp.where(kpos <= qpos, own, -jnp.inf))
    biases = [bias_t[j:j + 1, :] for j in range(n_past)] + [jnp.zeros((1, own.shape[1]), F32)]

    m = jnp.max(blocks[0], axis=0, keepdims=True) + biases[0]
    for blk_s, b in zip(blocks[1:], biases[1:]):
        m = jnp.maximum(m, jnp.max(blk_s, axis=0, keepdims=True) + b)
    probs = [jnp.exp2(blk_s - (m - b)) for blk_s, b in zip(blocks, biases)]
    l = jnp.sum(probs[0], axis=0, keepdims=True)
    for pj in probs[1:]:
        l = l + jnp.sum(pj, axis=0, keepdims=True)
    p_all = jnp.concatenate([pj.astype(BF16) for pj in probs], axis=0)
    acc = jnp.dot(vt_ref[:, 0:n_keys], p_all, preferred_element_type=F32)
    out = (acc * (1.0 / l)).T
    for g in range(KV_GROUP):
        o_ref[:, g * HEAD_DIM:(g + 1) * HEAD_DIM] = out[g * MOBA_BLOCK:(g + 1) * MOBA_BLOCK].astype(o_ref.dtype)


def _attn_kernel(q_ref, k_ref, v_ref, cq_ref, sq_ref, ck_ref, sk_ref, o_ref,
                 kr_ref, vt_ref, kg_ref, *, n_blocks):
    i = pl.program_id(2)
    scale = HEAD_DIM ** -0.5 * math.log2(math.e)
    gate_rows = 2 * V7X_SUBLANES

    @pl.when(i == 0)
    def _():
        kr = _rope(k_ref[...], ck_ref[...], sk_ref[...])
        kr_ref[:, :HEAD_DIM] = kr.astype(BF16)
        kr_ref[:, HEAD_DIM:] = jnp.zeros(kr.shape, BF16)
        vt_ref[...] = v_ref[...].T.astype(BF16)
        km = jnp.mean(kr.reshape(n_blocks, MOBA_BLOCK, HEAD_DIM), axis=1)
        km_hi, km_mid, _ = _split3_f32(km)
        zpad = jnp.zeros((gate_rows - n_blocks, 2 * HEAD_DIM), F32)
        kg_ref[...] = jnp.concatenate(
            [jnp.concatenate([km_hi, km_hi], axis=1), zpad,
             jnp.concatenate([km_mid, jnp.zeros_like(km_mid)], axis=1), zpad], axis=0).astype(BF16)

    cq = cq_ref[...]
    sq = sq_ref[...]
    q_all = jnp.concatenate(
        [_rope(q_ref[:, g * HEAD_DIM:(g + 1) * HEAD_DIM], cq, sq) * scale for g in range(KV_GROUP)],
        axis=0)
    q_hi = q_all.astype(BF16)
    q_mid = (q_all - q_hi.astype(F32)).astype(BF16)
    q_aug = jnp.concatenate([q_hi, q_mid], axis=1)

    for n_visible in range(1, n_blocks + 1):
        pl.when(i == n_visible - 1)(
            functools.partial(_attn_block_step, n_visible, q_aug, kr_ref, vt_ref, kg_ref, o_ref))


def _attention(proj, cos, sin, batch, seq):
    n_blocks = seq // MOBA_BLOCK
    assert n_blocks <= V7X_SUBLANES, "block gate layout holds at most 8 key blocks"
    qw = KV_GROUP * HEAD_DIM
    kern = functools.partial(_attn_kernel, n_blocks=n_blocks)
    return pl.pallas_call(
        kern,
        grid=(batch, N_KV_HEADS, n_blocks),
        in_specs=[
            pl.BlockSpec((MOBA_BLOCK, qw), lambda b, h, i: (b * n_blocks + i, h)),
            pl.BlockSpec((seq, HEAD_DIM), lambda b, h, i: (b, K_OFF // HEAD_DIM + h)),
            pl.BlockSpec((seq, HEAD_DIM), lambda b, h, i: (b, V_OFF // HEAD_DIM + h)),
            pl.BlockSpec((MOBA_BLOCK, HEAD_DIM), lambda b, h, i: (b * n_blocks + i, 0)),
            pl.BlockSpec((MOBA_BLOCK, HEAD_DIM), lambda b, h, i: (b * n_blocks + i, 0)),
            pl.BlockSpec((seq, HEAD_DIM), lambda b, h, i: (b, 0)),
            pl.BlockSpec((seq, HEAD_DIM), lambda b, h, i: (b, 0)),
        ],
        out_specs=pl.BlockSpec((MOBA_BLOCK, qw), lambda b, h, i: (b * n_blocks + i, h)),
        out_shape=jax.ShapeDtypeStruct((batch * seq, ATT_WIDTH), BF16),
        scratch_shapes=[pltpu.VMEM((seq, 2 * HEAD_DIM), BF16),
                        pltpu.VMEM((HEAD_DIM, seq), BF16),
                        pltpu.VMEM((4 * V7X_SUBLANES, 2 * HEAD_DIM), BF16)],
        compiler_params=_params(3),
        name="moba_attention",
    )(proj, proj, proj, cos, sin, cos, sin)


SSD_ROWS = 512
SSD_SUB = 128
N_EXPAND = 3


def _ssd_constants():
    hg = HEADS_PER_GROUP
    n_rows = N_EXPAND * 3 * hg
    r = jnp.arange(n_rows + (-n_rows) % (2 * V7X_SUBLANES))[:, None]
    c = jnp.arange(N_EXPAND * GROUP_WIDTH)[None, :]
    e = (r < n_rows) & (r // (3 * hg) == c // GROUP_WIDTH) & (r % hg == (c % GROUP_WIDTH) // SSM_HEAD_DIM)
    r2 = jnp.arange(6 * hg)[:, None]
    c2 = jnp.arange(hg * SSD_SUB)[None, :]
    d = (r2 % hg) == (c2 // SSD_SUB)
    return e.astype(BF16), d.astype(F32)


def _ssm_kernel(z_ref, xs_ref, bm_ref, cm_ref, dtT_ref,
                cwx_ref, cwb_ref, cwc_ref, cbx_ref, cbb_ref, cbc_ref,
                dtb_ref, alog_ref, dsk_ref, nw_ref, esel_ref, dsel_ref, o_ref,
                xt_ref, bt_ref, ct_ref, st_ref):
    c = pl.program_id(2)
    R = SSD_ROWS
    T = SSD_SUB
    hg = HEADS_PER_GROUP
    tail = V7X_SUBLANES
    tn = (((0,), (0,)), ((), ()))
    nt = (((1,), (1,)), ((), ()))

    @pl.when(c == 0)
    def _():
        for ext_ref in (xt_ref, bt_ref, ct_ref):
            ext_ref[0:tail, :] = jnp.zeros((tail, ext_ref.shape[1]), F32)
        st_ref[...] = jnp.zeros_like(st_ref)

    def conv_silu(u_ref, ext_ref, w_ref, b_ref):
        ext_ref[tail:tail + R, :] = u_ref[...]
        acc = b_ref[...]
        for k in range(CONV_WIDTH):
            lo = tail - (CONV_WIDTH - 1) + k
            acc = acc + w_ref[k:k + 1, :] * ext_ref[lo:lo + R, :]
        ext_ref[0:tail, :] = ext_ref[R:R + tail, :]
        return _silu_tanh(acc)

    xs = conv_silu(xs_ref, xt_ref, cwx_ref, cbx_ref)
    bm_bf = conv_silu(bm_ref, bt_ref, cwb_ref, cbb_ref).astype(BF16)
    cm_bf = conv_silu(cm_ref, ct_ref, cwc_ref, cbc_ref).astype(BF16)

    dt_in = dtT_ref[...] + dtb_ref[...]
    dt_t = jnp.maximum(dt_in, 0.0) + jnp.log1p(jnp.exp(-jnp.abs(dt_in)))
    adt_t = dt_t * (-jnp.exp(alog_ref[...]))

    lane = lax.broadcasted_iota(jnp.int32, (hg, T), 1)
    li = lax.broadcasted_iota(jnp.int32, (T, T), 0)
    si = lax.broadcasted_iota(jnp.int32, (T, T), 1)
    causal = si <= li
    half = lax.broadcasted_iota(jnp.int32, (T, V7X_LANES), 1) < SSM_HEAD_DIM
    ones = jnp.ones((hg, T), F32)
    esel = esel_ref[...]
    dsel = dsel_ref[...]

    st = st_ref[...]
    y_rows = []
    for sc in range(R // T):
        r0 = sc * T
        acum = adt_t[:, r0:r0 + T]
        shift = 1
        while shift < T:
            acum = acum + jnp.where(lane >= shift, pltpu.roll(acum, shift, 1), 0.0)
            shift *= 2
        dt_c = dt_t[:, r0:r0 + T]
        a_last = acum[:, T - 1:T]
        ecum = jnp.exp(acum)
        w = dt_c * jnp.exp(a_last - acum)

        parts = []
        for v in (dt_c, w, ecum):
            parts.extend(_split3_f32(v))
        pad_rows = esel.shape[0] - len(parts) * hg
        if pad_rows:
            parts.append(jnp.zeros((pad_rows, T), F32))
        stacked = jnp.concatenate(parts, axis=0).astype(BF16)
        expd = lax.dot_general(stacked, esel, tn, preferred_element_type=F32)
        dt_exp = expd[:, :GROUP_WIDTH]
        w_exp = expd[:, GROUP_WIDTH:2 * GROUP_WIDTH]
        ecum_exp = expd[:, 2 * GROUP_WIDTH:]

        a_hi, a_mid, a_lo = _split3_f32(acum * math.log2(math.e))
        lhs_t = jnp.concatenate([a_hi, a_mid, a_lo, ones, ones, ones], axis=0).astype(BF16)
        rhs = jnp.concatenate([ones, ones, ones, -a_hi, -a_mid, -a_lo], axis=0)
        rhs = (jnp.concatenate([rhs] * hg, axis=1) * dsel).astype(BF16)
        seg = lax.dot_general(lhs_t, rhs, tn, preferred_element_type=F32)

        xs_c = xs[r0:r0 + T]
        x_dt_bf = (xs_c * dt_exp).astype(BF16)
        bm_c = bm_bf[r0:r0 + T]
        cm_c = cm_bf[r0:r0 + T]
        cb = lax.dot_general(cm_c, bm_c, nt, preferred_element_type=F32)

        y_pairs = []
        for pr in range(hg // 2):
            mix = []
            for h in (2 * pr, 2 * pr + 1):
                dec = jnp.exp2(jnp.where(causal, seg[:, h * T:(h + 1) * T], -jnp.inf))
                mix.append((cb * dec).astype(BF16))
            x_pair = x_dt_bf[:, pr * V7X_LANES:(pr + 1) * V7X_LANES]
            zero = jnp.zeros_like(x_pair)
            x2 = jnp.concatenate([jnp.where(half, x_pair, zero), jnp.where(half, zero, x_pair)], axis=0)
            y_pairs.append(jnp.dot(jnp.concatenate(mix, axis=1), x2, preferred_element_type=F32))
        y_diag = jnp.concatenate(y_pairs, axis=1)

        y_off = jnp.dot(cm_c, st.astype(BF16), preferred_element_type=F32) * ecum_exp
        xw = (xs_c * w_exp).astype(BF16)
        states_t = lax.dot_general(bm_c, xw, tn, preferred_element_type=F32)
        st = st * ecum_exp[T - 1:T, :] + states_t
        y_rows.append(y_diag + y_off)

    st_ref[...] = st
    y = jnp.concatenate(y_rows, axis=0) + xs * dsk_ref[...]
    y = y * _silu_tanh(z_ref[...])
    y = y * lax.rsqrt(jnp.mean(y * y, axis=-1, keepdims=True) + EPS) * nw_ref[...]
    o_ref[...] = y.astype(o_ref.dtype)


def _ssm(proj, dt_t, conv_w, conv_b, dt_bias, a_log, d_skip, ssm_norm, batch, seq):
    R = SSD_ROWS
    nc = seq // R
    gw = GROUP_WIDTH
    hg = HEADS_PER_GROUP
    dsk = jnp.repeat(d_skip, SSM_HEAD_DIM)[None, :]
    esel, dsel = _ssd_constants()
    row = lambda b, g, c: b * nc + c
    in_specs = [
        pl.BlockSpec((R, gw), lambda b, g, c: (row(b, g, c), Z_OFF // gw + g)),
        pl.BlockSpec((R, gw), lambda b, g, c: (row(b, g, c), XS_OFF // gw + g)),
        pl.BlockSpec((R, SSM_STATE), lambda b, g, c: (row(b, g, c), BM_OFF // SSM_STATE + g)),
        pl.BlockSpec((R, SSM_STATE), lambda b, g, c: (row(b, g, c), CM_OFF // SSM_STATE + g)),
        pl.BlockSpec((hg, R), lambda b, g, c: (g, row(b, g, c))),
        pl.BlockSpec((CONV_WIDTH, gw), lambda b, g, c: (0, g)),
        pl.BlockSpec((CONV_WIDTH, SSM_STATE), lambda b, g, c: (0, SSM_WIDTH // SSM_STATE + g)),
        pl.BlockSpec((CONV_WIDTH, SSM_STATE), lambda b, g, c: (0, SSM_WIDTH // SSM_STATE + SSM_GROUPS + g)),
        pl.BlockSpec((1, gw), lambda b, g, c: (0, g)),
        pl.BlockSpec((1, SSM_STATE), lambda b, g, c: (0, SSM_WIDTH // SSM_STATE + g)),
        pl.BlockSpec((1, SSM_STATE), lambda b, g, c: (0, SSM_WIDTH // SSM_STATE + SSM_GROUPS + g)),
        pl.BlockSpec((hg, 1), lambda b, g, c: (g, 0)),
        pl.BlockSpec((hg, 1), lambda b, g, c: (g, 0)),
        pl.BlockSpec((1, gw), lambda b, g, c: (0, g)),
        pl.BlockSpec((1, gw), lambda b, g, c: (0, g)),
        pl.BlockSpec(esel.shape, lambda b, g, c: (0, 0)),
        pl.BlockSpec(dsel.shape, lambda b, g, c: (0, 0)),
    ]
    return pl.pallas_call(
        _ssm_kernel,
        grid=(batch, SSM_GROUPS, nc),
        in_specs=in_specs,
        out_specs=pl.BlockSpec((R, gw), lambda b, g, c: (row(b, g, c), g)),
        out_shape=jax.ShapeDtypeStruct((batch * seq, SSM_WIDTH), BF16),
        scratch_shapes=[pltpu.VMEM((V7X_SUBLANES + R, gw), F32),
                        pltpu.VMEM((V7X_SUBLANES + R, SSM_STATE), F32),
                        pltpu.VMEM((V7X_SUBLANES + R, SSM_STATE), F32),
                        pltpu.VMEM((SSM_STATE, gw), F32)],
        compiler_params=_params(3),
        name="ssd_mixer",
    )(proj, proj, proj, proj, dt_t,
      conv_w, conv_w, conv_w, conv_b, conv_b, conv_b,
      dt_bias.reshape(SSM_HEADS, 1), a_log.reshape(SSM_HEADS, 1), dsk, ssm_norm, esel, dsel)


def _outproj_kernel(att_ref, ssm_ref, wa_ref, ws_ref, x_ref, post_ref, x1_ref, *, tn, n_steps):
    n = pl.program_id(1)
    mixed = (jnp.dot(att_ref[...], wa_ref[...], preferred_element_type=F32)
             + jnp.dot(ssm_ref[...], ws_ref[...], preferred_element_type=F32))
    x1_ref[:, pl.ds(pl.multiple_of(n * tn, tn), tn)] = mixed

    @pl.when(n == n_steps - 1)
    def _():
        y = x1_ref[...]
        x1_ref[...] = x_ref[...] + y * lax.rsqrt(jnp.mean(y * y, axis=-1, keepdims=True) + EPS) * post_ref[...]


def _out_proj(att, ssm, w_out_bf16, x2d, post_w):
    n_tok = x2d.shape[0]
    tm, tn = 512, 512
    n_steps = D_MODEL // tn
    kern = functools.partial(_outproj_kernel, tn=tn, n_steps=n_steps)
    return pl.pallas_call(
        kern,
        grid=(n_tok // tm, n_steps),
        in_specs=[pl.BlockSpec((tm, ATT_WIDTH), lambda m, n: (m, 0)),
                  pl.BlockSpec((tm, SSM_WIDTH), lambda m, n: (m, 0)),
                  pl.BlockSpec((ATT_WIDTH, tn), lambda m, n: (0, n)),
                  pl.BlockSpec((SSM_WIDTH, tn), lambda m, n: (1, n)),
                  pl.BlockSpec((tm, D_MODEL), lambda m, n: (m, 0)),
                  pl.BlockSpec((1, D_MODEL), lambda m, n: (0, 0))],
        out_specs=pl.BlockSpec((tm, D_MODEL), lambda m, n: (m, 0)),
        out_shape=jax.ShapeDtypeStruct((n_tok, D_MODEL), F32),
        compiler_params=_params(2),
        name="out_proj",
    )(att, ssm, w_out_bf16, w_out_bf16, x2d, post_w)


FFN_TM = 512
FFN_TF = 256
FFN_FIRST_TF = 128


def _ffn_tile_step(f, n_steps, x1_ref, pre_ref, load_weights, post_ref, o_ref, h_ref):
    @pl.when(f == 0)
    def _():
        o_ref[...] = jnp.zeros_like(o_ref)
        x1 = x1_ref[...]
        h = x1 * lax.rsqrt(jnp.mean(x1 * x1, axis=-1, keepdims=True) + EPS) * pre_ref[...]
        h_ref[...] = h.astype(h_ref.dtype)

    wg, wu, wd = load_weights()
    h = h_ref[...]
    tf = wg.shape[1]
    if 2 * tf <= V7X_MXU_COLS:
        wgu = jnp.concatenate([wg, wu], axis=1)
        half_m = h.shape[0] // 2
        acts = []
        for r0 in (0, half_m):
            gu = jnp.dot(h[r0:r0 + half_m], wgu, preferred_element_type=F32)
            acts.append((_silu(gu[:, :tf]) * gu[:, tf:]).astype(BF16))
        act = jnp.concatenate(acts, axis=0)
    else:
        g = jnp.dot(h, wg, preferred_element_type=F32)
        u = jnp.dot(h, wu, preferred_element_type=F32)
        act = (_silu(g) * u).astype(BF16)
    o_ref[...] += jnp.dot(act, wd, preferred_element_type=F32)

    @pl.when(f == n_steps - 1)
    def _():
        y = o_ref[...]
        o_ref[...] = x1_ref[...] + y * lax.rsqrt(jnp.mean(y * y, axis=-1, keepdims=True) + EPS) * post_ref[...]


def _ffn_first_kernel(x1_ref, pre_ref, wg_ref, wu_ref, wd_ref, post_ref,
                      o_ref, wgb_ref, wub_ref, wdb_ref, h_ref, *, n_steps):
    def load_weights():
        wg = wg_ref[...].astype(BF16)
        wu = wu_ref[...].astype(BF16)
        wd = wd_ref[...].astype(BF16)
        wgb_ref[...] = wg
        wub_ref[...] = wu
        wdb_ref[...] = wd
        return wg, wu, wd

    _ffn_tile_step(pl.program_id(0), n_steps, x1_ref, pre_ref, load_weights, post_ref, o_ref, h_ref)


def _ffn_rest_kernel(x1_ref, pre_ref, wg_ref, wu_ref, wd_ref, post_ref, first_hbm, o_ref, h_ref, sem,
                     *, n_steps):
    m = pl.program_id(0)
    f = pl.program_id(1)

    @pl.when((m == 0) & (f == 0))
    def _():
        cp = pltpu.make_async_copy(first_hbm, o_ref, sem)
        cp.start()
        cp.wait()

    @pl.when(m > 0)
    def _():
        _ffn_tile_step(f, n_steps, x1_ref, pre_ref,
                       lambda: (wg_ref[...], wu_ref[...], wd_ref[...]), post_ref, o_ref, h_ref)


def _ffn(x1, pre_w, wg, wu, wd, post_w):
    n_tok = x1.shape[0]
    tm, tf, tf1 = FFN_TM, FFN_TF, FFN_FIRST_TF
    n_steps = D_FF // tf
    single = pl.Buffered(1)
    first, wg_bf, wu_bf, wd_bf = pl.pallas_call(
        functools.partial(_ffn_first_kernel, n_steps=D_FF // tf1),
        grid=(D_FF // tf1,),
        in_specs=[pl.BlockSpec((tm, D_MODEL), lambda f: (0, 0), pipeline_mode=single),
                  pl.BlockSpec((1, D_MODEL), lambda f: (0, 0)),
                  pl.BlockSpec((D_MODEL, tf1), lambda f: (0, f)),
                  pl.BlockSpec((D_MODEL, tf1), lambda f: (0, f)),
                  pl.BlockSpec((tf1, D_MODEL), lambda f: (f, 0)),
                  pl.BlockSpec((1, D_MODEL), lambda f: (0, 0))],
        out_specs=[pl.BlockSpec((tm, D_MODEL), lambda f: (0, 0), pipeline_mode=single),
                   pl.BlockSpec((D_MODEL, tf1), lambda f: (0, f)),
                   pl.BlockSpec((D_MODEL, tf1), lambda f: (0, f)),
                   pl.BlockSpec((tf1, D_MODEL), lambda f: (f, 0))],
        out_shape=[jax.ShapeDtypeStruct((tm, D_MODEL), F32),
                   jax.ShapeDtypeStruct(wg.shape, BF16),
                   jax.ShapeDtypeStruct(wu.shape, BF16),
                   jax.ShapeDtypeStruct(wd.shape, BF16)],
        scratch_shapes=[pltpu.VMEM((tm, D_MODEL), BF16)],
        compiler_params=_params(1),
        name="swiglu_ffn_first",
    )(x1, pre_w, wg, wu, wd, post_w)

    col = lambda m, f: (0, jnp.where(m == 0, 0, f))
    rowb = lambda m, f: (jnp.where(m == 0, 0, f), 0)
    return pl.pallas_call(
        functools.partial(_ffn_rest_kernel, n_steps=n_steps),
        grid=(n_tok // tm, n_steps),
        in_specs=[pl.BlockSpec((tm, D_MODEL), lambda m, f: (m, 0)),
                  pl.BlockSpec((1, D_MODEL), lambda m, f: (0, 0)),
                  pl.BlockSpec((D_MODEL, tf), col),
                  pl.BlockSpec((D_MODEL, tf), col),
                  pl.BlockSpec((tf, D_MODEL), rowb),
                  pl.BlockSpec((1, D_MODEL), lambda m, f: (0, 0)),
                  pl.BlockSpec(memory_space=pl.ANY)],
        out_specs=pl.BlockSpec((tm, D_MODEL), lambda m, f: (m, 0)),
        out_shape=jax.ShapeDtypeStruct((n_tok, D_MODEL), F32),
        scratch_shapes=[pltpu.VMEM((tm, D_MODEL), BF16), pltpu.SemaphoreType.DMA(())],
        compiler_params=_params(2),
        name="swiglu_ffn",
    )(x1, pre_w, wg_bf, wu_bf, wd_bf, post_w, first)


def kernel(x, positions, mix_pre_norm, w_in, conv_w, conv_b, dt_bias, a_log, d_skip, ssm_norm,
           w_out, mix_post_norm, ffn_pre_norm, w_gate, w_up, w_down, ffn_post_norm):
    batch, seq, _ = x.shape
    depth = w_in.shape[0]
    x2d = x.reshape(batch * seq, D_MODEL)
    cos, sin = _rope_tables(positions)
    for layer in range(depth):
        w_in_t = jnp.swapaxes(w_in[layer], 0, 1).astype(BF16)
        proj, dt_t = _in_proj(x2d, mix_pre_norm[layer][None, :], w_in_t)
        att = _attention(proj, cos, sin, batch, seq)
        ssm = _ssm(proj, dt_t, conv_w[layer], conv_b[layer][None, :], dt_bias[layer], a_log[layer],
                   d_skip[layer], ssm_norm[layer][None, :], batch, seq)
        x1 = _out_proj(att, ssm, w_out[layer].astype(BF16), x2d, mix_post_norm[layer][None, :])
        x2d = _ffn(x1, ffn_pre_norm[layer][None, :], w_gate[layer], w_up[layer], w_down[layer],
                   ffn_post_norm[layer][None, :])
    return x2d.reshape(batch, seq, D_MODEL)
```
